```python
import math
import jax, jax.numpy as jnp
from jax import lax
import numpy as np

D_MODEL = 2048
BATCH = 8
SEQ = 4096
DEPTH = 4

N_MIXERS = 2
N_MLA = (DEPTH + 1) // 2
N_DIFF = DEPTH // 2
Q_BLOCK = 128
PLE_DIM = 256

MLA_HEAD_DIM_NOPE = 128
MLA_HEAD_DIM_ROPE = 64
MLA_HEAD_DIM_V = 128
MLA_HEADS = D_MODEL // 128
MLA_Q_RANK = D_MODEL // 4
MLA_KV_RANK = D_MODEL // 4
ROPE_THETA = 10000.0

DIFF_HEAD_DIM = 128
DIFF_HEADS = D_MODEL // (2 * DIFF_HEAD_DIM)
DIFF_QK = 2 * DIFF_HEADS * DIFF_HEAD_DIM
DIFF_V = DIFF_HEADS * 2 * DIFF_HEAD_DIM

REL_BUCKETS = 32
REL_MAX_DIST = 128

D_FF = -(-8 * D_MODEL // (3 * 256)) * 256

ALPHA = (2 * DEPTH) ** 0.25
BETA = (8 * DEPTH) ** -0.25

LN_EPS = 1e-5
RMS_EPS = 1e-6

kernel_name = "hybrid_mla_diffattn_deepnorm_encoder"


def layer_norm(x, g, b):
    xf = x.astype(jnp.float32)
    mu = jnp.mean(xf, -1, keepdims=True)
    var = jnp.mean(jnp.square(xf - mu), -1, keepdims=True)
    return ((xf - mu) * lax.rsqrt(var + LN_EPS) * g.astype(jnp.float32)
            + b.astype(jnp.float32)).astype(x.dtype)


def rms_norm(x, g):
    xf = x.astype(jnp.float32)
    return (xf * lax.rsqrt(jnp.mean(xf * xf, -1, keepdims=True) + RMS_EPS)
            * g.astype(jnp.float32)).astype(x.dtype)


def rope_tables(seq, dtype):
    pos = jnp.arange(seq, dtype=jnp.float32)
    inv = 1.0 / (ROPE_THETA ** (jnp.arange(0, MLA_HEAD_DIM_ROPE, 2, dtype=jnp.float32) / MLA_HEAD_DIM_ROPE))
    ang = pos[:, None] * inv[None, :]
    return jnp.cos(ang).astype(dtype), jnp.sin(ang).astype(dtype)


def apply_rope(x, cos, sin):
    x1, x2 = jnp.split(x, 2, axis=-1)
    return jnp.concatenate([x1 * cos - x2 * sin, x2 * cos + x1 * sin], axis=-1)


def to_blocks(t):
    b, s = t.shape[:2]
    return jnp.moveaxis(t.reshape((b, s // Q_BLOCK, Q_BLOCK) + t.shape[2:]), 1, 0)


def from_blocks(t):
    t = jnp.moveaxis(t, 0, 1)
    return t.reshape((t.shape[0], t.shape[1] * t.shape[2]) + t.shape[3:])


def t5_bucket(rel):
    nb = REL_BUCKETS // 2
    max_exact = nb // 2
    ret = (rel > 0).astype(jnp.int32) * nb
    n = jnp.abs(rel)
    nf = jnp.maximum(n, 1).astype(jnp.float32)
    large = max_exact + (jnp.log(nf / max_exact) / math.log(REL_MAX_DIST / max_exact)
                         * (nb - max_exact)).astype(jnp.int32)
    large = jnp.minimum(large, nb - 1)
    return ret + jnp.where(n < max_exact, n, large)


def mla_mixer(x, w_in, q_norm, kv_norm, w_uq, w_ukv, w_o, cos, sin):
    b, s, _ = x.shape
    h = x @ w_in
    c_q, c_kv, k_rope = jnp.split(h, [MLA_Q_RANK, MLA_Q_RANK + MLA_KV_RANK], axis=-1)
    q = (rms_norm(c_q, q_norm) @ w_uq).reshape(b, s, MLA_HEADS, MLA_HEAD_DIM_NOPE + MLA_HEAD_DIM_ROPE)
    q_nope, q_rope = q[..., :MLA_HEAD_DIM_NOPE], q[..., MLA_HEAD_DIM_NOPE:]
    q_rope = apply_rope(q_rope, cos[None, :, None], sin[None, :, None])
    kv = (rms_norm(c_kv, kv_norm) @ w_ukv).reshape(b, s, MLA_HEADS, MLA_HEAD_DIM_NOPE + MLA_HEAD_DIM_V)
    k_nope, v = kv[..., :MLA_HEAD_DIM_NOPE], kv[..., MLA_HEAD_DIM_NOPE:]
    k_rope = apply_rope(k_rope, cos[None], sin[None])
    scale = (MLA_HEAD_DIM_NOPE + MLA_HEAD_DIM_ROPE) ** -0.5

    def attend(blk):
        qn, qr = blk
        logits = (jnp.einsum('bqhd,bkhd->bhqk', qn, k_nope)
                  + jnp.einsum('bqhr,bkr->bhqk', qr, k_rope)).astype(jnp.float32) * scale
        probs = jax.nn.softmax(logits, axis=-1).astype(v.dtype)
        return jnp.einsum('bhqk,bkhd->bqhd', probs, v)

    o = from_blocks(lax.map(attend, (to_blocks(q_nope), to_blocks(q_rope))))
    return o.reshape(b, s, MLA_HEADS * MLA_HEAD_DIM_V) @ w_o


def diff_mixer(x, w_in, lam, sub_norm, w_o, rel_bias, layer_idx):
    b, s, _ = x.shape
    h = x @ w_in
    q = h[..., :DIFF_QK].reshape(b, s, 2 * DIFF_HEADS, DIFF_HEAD_DIM)
    k = h[..., DIFF_QK:2 * DIFF_QK].reshape(b, s, 2 * DIFF_HEADS, DIFF_HEAD_DIM)
    v = h[..., 2 * DIFF_QK:].reshape(b, s, DIFF_HEADS, 2 * DIFF_HEAD_DIM)
    lambda_init = 0.8 - 0.6 * math.exp(-0.3 * layer_idx)
    lf = lam.astype(jnp.float32)
    lam_full = jnp.exp(jnp.sum(lf[0] * lf[1])) - jnp.exp(jnp.sum(lf[2] * lf[3])) + lambda_init
    scale = DIFF_HEAD_DIM ** -0.5
    key_pos = jnp.arange(s, dtype=jnp.int32)
    starts = jnp.arange(s // Q_BLOCK, dtype=jnp.int32) * Q_BLOCK
    table = rel_bias.astype(jnp.float32)

    def attend(blk):
        qb, start = blk
        logits = jnp.einsum('bqgd,bkgd->bgqk', qb, k).astype(jnp.float32) * scale
        logits = logits.reshape(b, DIFF_HEADS, 2, Q_BLOCK, s)
        rel = key_pos[None, :] - (start + jnp.arange(Q_BLOCK, dtype=jnp.int32))[:, None]
        bias = jnp.moveaxis(table[t5_bucket(rel)], -1, 0)
        probs = jax.nn.softmax(logits + bias[None, :, None], axis=-1)
        diff = (probs[:, :, 0] - lam_full * probs[:, :, 1]).astype(v.dtype)
        return jnp.einsum('bhqk,bkhe->bqhe', diff, v)

    o = from_blocks(lax.map(attend, (to_blocks(q), starts)))
    o = rms_norm(o, sub_norm) * (1.0 - lambda_init)
    return o.reshape(b, s, DIFF_V) @ w_o


def swiglu(x, w_in, w_out):
    g, u = jnp.split(x @ w_in, 2, axis=-1)
    return (jax.nn.silu(g) * u) @ w_out


def setup_inputs(seed: int = 0) -> dict:
    key = jax.random.key(seed)
    ks = jax.random.split(key, 21)
    f32 = jnp.float32

    def nrm(k, shape, scale):
        return jax.random.normal(k, shape, f32) * scale

    mla_in_w = MLA_Q_RANK + MLA_KV_RANK + MLA_HEAD_DIM_ROPE
    return {
        "x": nrm(ks[0], (BATCH, SEQ, D_MODEL), 1.0),
        "p": nrm(ks[1], (DEPTH, BATCH, SEQ, PLE_DIM), 1.0),
        "mla_w_in": nrm(ks[2], (N_MLA, D_MODEL, mla_in_w), D_MODEL ** -0.5),
        "mla_q_norm": 1.0 + nrm(ks[3], (N_MLA, MLA_Q_RANK), 0.02),
        "mla_kv_norm": 1.0 + nrm(ks[4], (N_MLA, MLA_KV_RANK), 0.02),
        "mla_w_uq": nrm(ks[5], (N_MLA, MLA_Q_RANK, MLA_HEADS * (MLA_HEAD_DIM_NOPE + MLA_HEAD_DIM_ROPE)), MLA_Q_RANK ** -0.5),
        "mla_w_ukv": nrm(ks[6], (N_MLA, MLA_KV_RANK, MLA_HEADS * (MLA_HEAD_DIM_NOPE + MLA_HEAD_DIM_V)), MLA_KV_RANK ** -0.5),
        "mla_w_o": nrm(ks[7], (N_MLA, MLA_HEADS * MLA_HEAD_DIM_V, D_MODEL), BETA * (MLA_HEADS * MLA_HEAD_DIM_V) ** -0.5),
        "diff_w_in": nrm(ks[8], (N_DIFF, D_MODEL, 2 * DIFF_QK + DIFF_V), D_MODEL ** -0.5),
        "diff_lambda": nrm(ks[9], (N_DIFF, 4, DIFF_HEAD_DIM), 0.1),
        "diff_sub_norm": 1.0 + nrm(ks[10], (N_DIFF, 2 * DIFF_HEAD_DIM), 0.02),
        "diff_w_o": nrm(ks[11], (N_DIFF, DIFF_V, D_MODEL), BETA * DIFF_V ** -0.5),
        "rel_bias": nrm(ks[12], (REL_BUCKETS, DIFF_HEADS), 0.5),
        "ln_g": 1.0 + nrm(ks[13], (DEPTH, 2, D_MODEL), 0.02),
        "ln_b": nrm(ks[14], (DEPTH, 2, D_MODEL), 0.02),
        "ffn_w_in": nrm(ks[15], (DEPTH, D_MODEL, 2 * D_FF), D_MODEL ** -0.5),
        "ffn_w_out": nrm(ks[16], (DEPTH, D_FF, D_MODEL), BETA * D_FF ** -0.5),
        "ple_w_gate": nrm(ks[17], (DEPTH, D_MODEL, D_MODEL), D_MODEL ** -0.5),
        "ple_w_proj": nrm(ks[18], (DEPTH, PLE_DIM, D_MODEL), PLE_DIM ** -0.5),
    }


def reference(x, p, mla_w_in, mla_q_norm, mla_kv_norm, mla_w_uq, mla_w_ukv, mla_w_o,
              diff_w_in, diff_lambda, diff_sub_norm, diff_w_o, rel_bias,
              ln_g, ln_b, ffn_w_in, ffn_w_out, ple_w_gate, ple_w_proj):
    cos, sin = rope_tables(x.shape[1], x.dtype)
    for i in range(DEPTH):
        j = i // N_MIXERS
        if i % N_MIXERS == 0:
            mix = mla_mixer(x, mla_w_in[j], mla_q_norm[j], mla_kv_norm[j],
                            mla_w_uq[j], mla_w_ukv[j], mla_w_o[j], cos, sin)
        else:
            mix = diff_mixer(x, diff_w_in[j], diff_lambda[j], diff_sub_norm[j],
                             diff_w_o[j], rel_bias, i)
        x = layer_norm(ALPHA * x + mix, ln_g[i, 0], ln_b[i, 0])
        x = layer_norm(ALPHA * x + swiglu(x, ffn_w_in[i], ffn_w_out[i]), ln_g[i, 1], ln_b[i, 1])
        x = x + jax.nn.sigmoid(x @ ple_w_gate[i]) * (p[i] @ ple_w_proj[i])
    return x
```

```python
import functools
import math

import jax
import jax.numpy as jnp
from jax import lax
from jax.experimental import pallas as pl
from jax.experimental.pallas import tpu as pltpu

LANES_V7X = 128
VMEM_BYTES_V7X = 64 * 1024 * 1024
VMEM_LIMIT_CAP = VMEM_BYTES_V7X - 8 * 1024 * 1024

HEAD_DIM = 128
ROPE_DIM = 64
ROPE_THETA = 10000.0
REL_BUCKETS = 32
REL_MAX_DIST = 128
LN_EPS = 1e-5
RMS_EPS = 1e-6
LOG2E = math.log2(math.e)

BF16 = jnp.bfloat16
F32 = jnp.float32


def _vmem_limit(block_bytes):
    return int(min(VMEM_LIMIT_CAP, max(32 * 1024 * 1024, 2 * block_bytes + 16 * 1024 * 1024)))


def _nbytes(shape, dtype):
    return math.prod(shape) * jnp.dtype(dtype).itemsize


def _params(semantics, block_bytes):
    return pltpu.CompilerParams(dimension_semantics=semantics,
                                vmem_limit_bytes=_vmem_limit(block_bytes))


def _tile(n, want):
    if n <= want:
        return n
    t = want - want % LANES_V7X
    while t > LANES_V7X and n % t:
        t -= LANES_V7X
    assert n % t == 0, (n, want)
    return t


def _dot(a, b):
    return jnp.dot(a, b, preferred_element_type=F32)


def _dot_nt(a, b):
    return lax.dot_general(a, b, (((1,), (1,)), ((), ())), preferred_element_type=F32)


def _layer_norm(y, g, b):
    mu = jnp.mean(y, axis=-1, keepdims=True)
    yc = y - mu
    var = jnp.mean(yc * yc, axis=-1, keepdims=True)
    return yc * lax.rsqrt(var + LN_EPS) * g + b


def _rms_norm(y, g):
    return y * lax.rsqrt(jnp.mean(y * y, axis=-1, keepdims=True) + RMS_EPS) * g


def _rope_half(r, cos_t, sin_a, sin_b):
    up = pltpu.roll(r, LANES_V7X - ROPE_DIM // 2, axis=1)
    down = pltpu.roll(r, ROPE_DIM // 2, axis=1)
    return r * cos_t + up * sin_a + down * sin_b


def _matmul_kernel(x_ref, w_ref, o_ref):
    o_ref[...] = _dot(x_ref[...], w_ref[...]).astype(o_ref.dtype)


def _matmul(x, w, out_dtype, tm=1024, tn=1024):
    m, k = x.shape
    n = w.shape[1]
    tm, tn = _tile(m, tm), _tile(n, tn)
    blocks = _nbytes((tm, k), x.dtype) + _nbytes((k, tn), w.dtype) + _nbytes((tm, tn), out_dtype)
    return pl.pallas_call(
        _matmul_kernel,
        grid=(m // tm, n // tn),
        in_specs=[pl.BlockSpec((tm, k), lambda i, j: (i, 0)),
                  pl.BlockSpec((k, tn), lambda i, j: (0, j))],
        out_specs=pl.BlockSpec((tm, tn), lambda i, j: (i, j)),
        out_shape=jax.ShapeDtypeStruct((m, n), out_dtype),
        compiler_params=_params(("parallel", "arbitrary"), blocks + _nbytes((tm, tn), F32)),
        name="matmul",
    )(x, w)


def _mla_in_kernel(x_ref, w_ref, gq_ref, gkv_ref, cos_ref, sa_ref, sb_ref,
                   cq_ref, ckv_ref, kr_ref, *, q_rank, kv_rank):
    h = _dot(x_ref[...], w_ref[...])
    cq_ref[...] = _rms_norm(h[:, :q_rank], gq_ref[...]).astype(cq_ref.dtype)
    ckv_ref[...] = _rms_norm(h[:, q_rank:q_rank + kv_rank], gkv_ref[...]).astype(ckv_ref.dtype)
    kr = h[:, q_rank + kv_rank:]
    kr_ref[...] = _rope_half(kr, cos_ref[...], sa_ref[...], sb_ref[...]).astype(kr_ref.dtype)


def _mla_in(xb, w_cat, gq, gkv, rope_tabs, seq, tm=1024):
    m, k = xb.shape
    q_rank, kv_rank = gq.shape[1], gkv.shape[1]
    n = w_cat.shape[1]
    tm = _tile(seq, tm)
    per_seq = seq // tm
    blocks = (_nbytes((tm, k), BF16) + _nbytes((k, n), BF16) + 3 * _nbytes((tm, LANES_V7X), F32)
              + _nbytes((tm, n), BF16))
    tab_spec = pl.BlockSpec((tm, LANES_V7X), lambda i: (i % per_seq, 0))
    return pl.pallas_call(
        functools.partial(_mla_in_kernel, q_rank=q_rank, kv_rank=kv_rank),
        grid=(m // tm,),
        in_specs=[pl.BlockSpec((tm, k), lambda i: (i, 0)),
                  pl.BlockSpec((k, n), lambda i: (0, 0)),
                  pl.BlockSpec((1, q_rank), lambda i: (0, 0)),
                  pl.BlockSpec((1, kv_rank), lambda i: (0, 0)),
                  tab_spec, tab_spec, tab_spec],
        out_specs=[pl.BlockSpec((tm, q_rank), lambda i: (i, 0)),
                   pl.BlockSpec((tm, kv_rank), lambda i: (i, 0)),
                   pl.BlockSpec((tm, LANES_V7X), lambda i: (i, 0))],
        out_shape=[jax.ShapeDtypeStruct((m, q_rank), BF16),
                   jax.ShapeDtypeStruct((m, kv_rank), BF16),
                   jax.ShapeDtypeStruct((m, LANES_V7X), BF16)],
        compiler_params=_params(("parallel",), blocks + _nbytes((tm, n), F32)),
        name="mla_in",
    )(xb, w_cat, gq, gkv, *rope_tabs)


def _mla_q_kernel(c_ref, w_ref, cos_ref, sa_ref, sb_ref, q_ref, *, heads_per_step):
    acc = _dot(c_ref[...], w_ref[...])
    cos_t, sa, sb = cos_ref[...], sa_ref[...], sb_ref[...]
    for h in range(heads_per_step):
        base = 2 * HEAD_DIM * h
        q_ref[:, base:base + HEAD_DIM] = acc[:, base:base + HEAD_DIM].astype(q_ref.dtype)
        rope = _rope_half(acc[:, base + HEAD_DIM:base + 2 * HEAD_DIM], cos_t, sa, sb)
        q_ref[:, base + HEAD_DIM:base + 2 * HEAD_DIM] = rope.astype(q_ref.dtype)


def _mla_q(cq, w_uq, rope_tabs, seq, tm=1024, heads_per_step=4):
    m, k = cq.shape
    n = w_uq.shape[1]
    tm = _tile(seq, tm)
    per_seq = seq // tm
    tn = min(n, heads_per_step * 2 * HEAD_DIM)
    heads_per_step = tn // (2 * HEAD_DIM)
    blocks = (_nbytes((tm, k), BF16) + _nbytes((k, tn), BF16) + 3 * _nbytes((tm, LANES_V7X), F32)
              + _nbytes((tm, tn), BF16))
    tab_spec = pl.BlockSpec((tm, LANES_V7X), lambda i, j: (i % per_seq, 0))
    return pl.pallas_call(
        functools.partial(_mla_q_kernel, heads_per_step=heads_per_step),
        grid=(m // tm, n // tn),
        in_specs=[pl.BlockSpec((tm, k), lambda i, j: (i, 0)),
                  pl.BlockSpec((k, tn), lambda i, j: (0, j)),
                  tab_spec, tab_spec, tab_spec],
        out_specs=pl.BlockSpec((tm, tn), lambda i, j: (i, j)),
        out_shape=jax.ShapeDtypeStruct((m, n), BF16),
        compiler_params=_params(("parallel", "arbitrary"), blocks + _nbytes((tm, tn), F32)),
        name="mla_q",
    )(cq, w_uq, *rope_tabs)


def _mla_kv_kernel(c_ref, wk_ref, wv_ref, kr_ref, k_ref, v_ref, *, heads_per_step):
    c = c_ref[...]
    kn = _dot(c, wk_ref[...])
    v_ref[...] = _dot(c, wv_ref[...]).astype(v_ref.dtype)
    kr = kr_ref[...]
    for h in range(heads_per_step):
        base = 2 * HEAD_DIM * h
        k_ref[:, base:base + HEAD_DIM] = kn[:, HEAD_DIM * h:HEAD_DIM * (h + 1)].astype(k_ref.dtype)
        k_ref[:, base + HEAD_DIM:base + 2 * HEAD_DIM] = kr


def _mla_kv(ckv, w_k, w_v, kr, tm=1024, heads_per_step=4):
    m, k = ckv.shape
    n = w_k.shape[1]
    tm = _tile(m, tm)
    tn = min(n, heads_per_step * HEAD_DIM)
    heads_per_step = tn // HEAD_DIM
    blocks = (_nbytes((tm, k), BF16) + 2 * _nbytes((k, tn), BF16) + _nbytes((tm, LANES_V7X), BF16)
              + 3 * _nbytes((tm, tn), BF16))
    return pl.pallas_call(
        functools.partial(_mla_kv_kernel, heads_per_step=heads_per_step),
        grid=(m // tm, n // tn),
        in_specs=[pl.BlockSpec((tm, k), lambda i, j: (i, 0)),
                  pl.BlockSpec((k, tn), lambda i, j: (0, j)),
                  pl.BlockSpec((k, tn), lambda i, j: (0, j)),
                  pl.BlockSpec((tm, LANES_V7X), lambda i, j: (i, 0))],
        out_specs=[pl.BlockSpec((tm, 2 * tn), lambda i, j: (i, j)),
                   pl.BlockSpec((tm, tn), lambda i, j: (i, j))],
        out_shape=[jax.ShapeDtypeStruct((m, 2 * n), BF16),
                   jax.ShapeDtypeStruct((m, n), BF16)],
        compiler_params=_params(("parallel", "arbitrary"), blocks + 2 * _nbytes((tm, tn), F32)),
        name="mla_kv",
    )(ckv, w_k, w_v, kr)


def _softmax_pv(q, k_ref, v_ref, s_ref, k_cols, tk, c1, bias_fn=None):
    tq = q.shape[0]
    seq = k_ref.shape[0]
    n_chunks = seq // tk
    groups = tk // LANES_V7X
    m_acc = jnp.full((tq, LANES_V7X), -jnp.inf, F32)
    for c in range(n_chunks):
        s = _dot_nt(q, k_ref[c * tk:(c + 1) * tk, k_cols]) * c1
        if bias_fn is not None:
            s = s + bias_fn(c)
        s_ref[:, c * tk:(c + 1) * tk] = s
        for g in range(groups):
            m_acc = jnp.maximum(m_acc, s[:, g * LANES_V7X:(g + 1) * LANES_V7X])
    m = jnp.max(m_acc, axis=-1, keepdims=True)
    l_acc = jnp.zeros((tq, LANES_V7X), F32)
    acc = jnp.zeros((tq, v_ref.shape[1]), F32)
    for c in range(n_chunks):
        p = jnp.exp2(s_ref[:, c * tk:(c + 1) * tk] - m)
        for g in range(groups):
            l_acc = l_acc + p[:, g * LANES_V7X:(g + 1) * LANES_V7X]
        acc = acc + _dot(p.astype(BF16), v_ref[c * tk:(c + 1) * tk, :])
    return acc, jnp.sum(l_acc, axis=-1, keepdims=True)


def _mla_attn_kernel(q_ref, k_ref, v_ref, o_ref, s_ref, *, tk, c1):
    acc, l = _softmax_pv(q_ref[...], k_ref, v_ref, s_ref, slice(None), tk, c1)
    o_ref[...] = (acc / l).astype(o_ref.dtype)


def _mla_attn(q, k, v, batch, seq, heads, tq=512, tk=512):
    tq, tk = _tile(seq, tq), _tile(seq, tk)
    nq = seq // tq
    dk, dv = 2 * HEAD_DIM, HEAD_DIM
    c1 = (HEAD_DIM + ROPE_DIM) ** -0.5 * LOG2E
    blocks = (_nbytes((tq, dk), BF16) + _nbytes((seq, dk), BF16) + _nbytes((seq, dv), BF16)
              + _nbytes((tq, dv), BF16))
    return pl.pallas_call(
        functools.partial(_mla_attn_kernel, tk=tk, c1=c1),
        grid=(batch, heads, nq),
        in_specs=[pl.BlockSpec((tq, dk), lambda b, h, i: (b * nq + i, h)),
                  pl.BlockSpec((seq, dk), lambda b, h, i: (b, h)),
                  pl.BlockSpec((seq, dv), lambda b, h, i: (b, h))],
        out_specs=pl.BlockSpec((tq, dv), lambda b, h, i: (b * nq + i, h)),
        out_shape=jax.ShapeDtypeStruct((batch * seq, heads * dv), BF16),
        scratch_shapes=[pltpu.VMEM((tq, seq), F32)],
        compiler_params=_params(("parallel", "parallel", "arbitrary"),
                                blocks + _nbytes((tq, seq), F32)),
        name="mla_attn",
    )(q, k, v)


def _bias_master_kernel(vec_ref, o_ref, *, tq):
    width = vec_ref.shape[-1]
    m = jnp.broadcast_to(vec_ref[0], (tq, width))
    left = (tq - 1) - lax.broadcasted_iota(jnp.int32, (tq, 1), 0)
    step = 1
    while step < tq:
        shifted = pltpu.roll(m, width - step, axis=1)
        m = jnp.where((left & step) != 0, shifted, m)
        step *= 2
    o_ref[0] = m


def _bias_master(vec, tq):
    heads, _, width = vec.shape
    blocks = _nbytes((1, width), F32) + _nbytes((tq, width), F32)
    return pl.pallas_call(
        functools.partial(_bias_master_kernel, tq=tq),
        grid=(heads,),
        in_specs=[pl.BlockSpec((1, 1, width), lambda h: (h, 0, 0))],
        out_specs=pl.BlockSpec((1, tq, width), lambda h: (h, 0, 0)),
        out_shape=jax.ShapeDtypeStruct((heads, tq, width), F32),
        compiler_params=_params(("parallel",), blocks + 2 * _nbytes((tq, width), F32)),
        name="bias_master",
    )(vec)


def _diff_attn_kernel(lam_ref, g_ref, q_ref, k_ref, v_ref, bias_ref, o_ref, s_ref,
                      *, tk, c1, lambda_init):
    tq = q_ref.shape[0]
    seq = k_ref.shape[0]
    lam = lam_ref[...]
    lam_full = (jnp.exp(jnp.sum(lam[0:1] * lam[1:2], axis=-1, keepdims=True))
                - jnp.exp(jnp.sum(lam[2:3] * lam[3:4], axis=-1, keepdims=True)) + lambda_init)
    off = pl.multiple_of(seq - tq - pl.program_id(2) * tq, LANES_V7X)

    def bias_fn(c):
        return bias_ref[0, :, pl.ds(pl.multiple_of(off + c * tk, LANES_V7X), tk)]

    outs = []
    for g in range(2):
        cols = slice(g * HEAD_DIM, (g + 1) * HEAD_DIM)
        acc, l = _softmax_pv(q_ref[:, cols], k_ref, v_ref, s_ref, cols, tk, c1, bias_fn)
        outs.append(acc / l)
    o = outs[0] - lam_full * outs[1]
    o = _rms_norm(o, g_ref[...]) * (1.0 - lambda_init)
    o_ref[...] = o.astype(o_ref.dtype)


def _diff_attn(qkv, lam, sub_norm, bias_master, batch, seq, heads, layer_idx, tq, tk=512):
    tk = _tile(seq, tk)
    nq = seq // tq
    d2 = 2 * HEAD_DIM
    lambda_init = 0.8 - 0.6 * math.exp(-0.3 * layer_idx)
    c1 = HEAD_DIM ** -0.5 * LOG2E
    blocks = (2 * _nbytes((tq, d2), BF16) + 2 * _nbytes((seq, d2), BF16)
              + _nbytes((tq, 2 * seq), F32))
    return pl.pallas_call(
        functools.partial(_diff_attn_kernel, tk=tk, c1=c1, lambda_init=lambda_init),
        grid=(heads, batch, nq),
        in_specs=[pl.BlockSpec(lam.shape, lambda h, b, i: (0, 0)),
                  pl.BlockSpec((1, d2), lambda h, b, i: (0, 0)),
                  pl.BlockSpec((tq, d2), lambda h, b, i: (b * nq + i, h)),
                  pl.BlockSpec((seq, d2), lambda h, b, i: (b, heads + h)),
                  pl.BlockSpec((seq, d2), lambda h, b, i: (b, 2 * heads + h)),
                  pl.BlockSpec((1, tq, 2 * seq), lambda h, b, i: (h, 0, 0))],
        out_specs=pl.BlockSpec((tq, d2), lambda h, b, i: (b * nq + i, h)),
        out_shape=jax.ShapeDtypeStruct((batch * seq, heads * d2), BF16),
        scratch_shapes=[pltpu.VMEM((tq, seq), F32)],
        compiler_params=_params(("parallel", "parallel", "arbitrary"),
                                blocks + _nbytes((tq, seq), F32)),
        name="diff_attn",
    )(lam, sub_norm, qkv, qkv, qkv, bias_master)


def _proj_ln_kernel(a_ref, w_ref, x_ref, g_ref, b_ref, y_ref, yb_ref, *, alpha):
    y = alpha * x_ref[...] + _dot(a_ref[...], w_ref[...])
    y = _layer_norm(y, g_ref[...], b_ref[...])
    y_ref[...] = y
    yb_ref[...] = y.astype(yb_ref.dtype)


def _proj_ln(a, w, x, g, b, alpha, tm=512):
    m, k = a.shape
    d = w.shape[1]
    tm = _tile(m, tm)
    blocks = (_nbytes((tm, k), BF16) + _nbytes((k, d), BF16) + 2 * _nbytes((tm, d), F32)
              + _nbytes((tm, d), BF16))
    row = lambda i: (i, 0)
    fixed = lambda i: (0, 0)
    return pl.pallas_call(
        functools.partial(_proj_ln_kernel, alpha=alpha),
        grid=(m // tm,),
        in_specs=[pl.BlockSpec((tm, k), row), pl.BlockSpec((k, d), fixed),
                  pl.BlockSpec((tm, d), row), pl.BlockSpec((1, d), fixed),
                  pl.BlockSpec((1, d), fixed)],
        out_specs=[pl.BlockSpec((tm, d), row), pl.BlockSpec((tm, d), row)],
        out_shape=[jax.ShapeDtypeStruct((m, d), F32), jax.ShapeDtypeStruct((m, d), BF16)],
        compiler_params=_params(("parallel",), blocks + 2 * _nbytes((tm, d), F32)),
        name="proj_ln",
    )(a, w, x, g, b)


def _ffn_up_kernel(x_ref, wg_ref, wu_ref, h_ref):
    x = x_ref[...]
    g = _dot(x, wg_ref[...])
    u = _dot(x, wu_ref[...])
    h_ref[...] = (g * jax.nn.sigmoid(g) * u).astype(h_ref.dtype)


def _ffn_up(xb, w_in, tm=1024, tn=512):
    m, k = xb.shape
    ff = w_in.shape[1] // 2
    tm, tn = _tile(m, tm), _tile(ff, tn)
    n_tiles = ff // tn
    blocks = _nbytes((tm, k), BF16) + 2 * _nbytes((k, tn), BF16) + _nbytes((tm, tn), BF16)
    return pl.pallas_call(
        _ffn_up_kernel,
        grid=(m // tm, n_tiles),
        in_specs=[pl.BlockSpec((tm, k), lambda i, j: (i, 0)),
                  pl.BlockSpec((k, tn), lambda i, j: (0, j)),
                  pl.BlockSpec((k, tn), lambda i, j: (0, j + n_tiles))],
        out_specs=pl.BlockSpec((tm, tn), lambda i, j: (i, j)),
        out_shape=jax.ShapeDtypeStruct((m, ff), BF16),
        compiler_params=_params(("parallel", "arbitrary"), blocks + 3 * _nbytes((tm, tn), F32)),
        name="ffn_up",
    )(xb, w_in, w_in)


def _ffn_down_ln_kernel(h_ref, w_ref, x_ref, g_ref, b_ref, y_ref, yb_ref, *, alpha):
    kk = pl.program_id(1)

    @pl.when(kk == 0)
    def _():
        y_ref[...] = alpha * x_ref[...]

    y_ref[...] += _dot(h_ref[...], w_ref[...])

    @pl.when(kk == pl.num_programs(1) - 1)
    def _():
        y = _layer_norm(y_ref[...], g_ref[...], b_ref[...])
        y_ref[...] = y
        yb_ref[...] = y.astype(yb_ref.dtype)


def _ffn_down_ln(h, w, x, g, b, alpha, tm=512, tk=1408):
    m, ff = h.shape
    d = w.shape[1]
    tm, tk = _tile(m, tm), _tile(ff, tk)
    blocks = (_nbytes((tm, tk), BF16) + _nbytes((tk, d), BF16) + 2 * _nbytes((tm, d), F32)
              + _nbytes((tm, d), BF16))
    row = lambda i, kk: (i, 0)
    fixed = lambda i, kk: (0, 0)
    return pl.pallas_call(
        functools.partial(_ffn_down_ln_kernel, alpha=alpha),
        grid=(m // tm, ff // tk),
        in_specs=[pl.BlockSpec((tm, tk), lambda i, kk: (i, kk)),
                  pl.BlockSpec((tk, d), lambda i, kk: (kk, 0)),
                  pl.BlockSpec((tm, d), row), pl.BlockSpec((1, d), fixed),
                  pl.BlockSpec((1, d), fixed)],
        out_specs=[pl.BlockSpec((tm, d), row), pl.BlockSpec((tm, d), row)],
        out_shape=[jax.ShapeDtypeStruct((m, d), F32), jax.ShapeDtypeStruct((m, d), BF16)],
        compiler_params=_params(("parallel", "arbitrary"), blocks + _nbytes((tm, d), F32)),
        name="ffn_down_ln",
    )(h, w, x, g, b)


def _ple_kernel(xb_ref, wg_ref, p_ref, wp_ref, x_ref, y_ref, yb_ref):
    gate = jax.nn.sigmoid(_dot(xb_ref[...], wg_ref[...]))
    proj = _dot(p_ref[...].astype(BF16), wp_ref[...])
    y = x_ref[...] + gate * proj
    y_ref[...] = y
    yb_ref[...] = y.astype(yb_ref.dtype)


def _ple(xb, x, p, w_gate, w_proj, tm=1024, tn=1024):
    m, d = x.shape
    pd = p.shape[1]
    tm, tn = _tile(m, tm), _tile(d, tn)
    blocks = (_nbytes((tm, d), BF16) + _nbytes((d, tn), BF16) + _nbytes((tm, pd), F32)
              + _nbytes((pd, tn), BF16) + 2 * _nbytes((tm, tn), F32) + _nbytes((tm, tn), BF16))
    return pl.pallas_call(
        _ple_kernel,
        grid=(m // tm, d // tn),
        in_specs=[pl.BlockSpec((tm, d), lambda i, j: (i, 0)),
                  pl.BlockSpec((d, tn), lambda i, j: (0, j)),
                  pl.BlockSpec((tm, pd), lambda i, j: (i, 0)),
                  pl.BlockSpec((pd, tn), lambda i, j: (0, j)),
                  pl.BlockSpec((tm, tn), lambda i, j: (i, j))],
        out_specs=[pl.BlockSpec((tm, tn), lambda i, j: (i, j)),
                   pl.BlockSpec((tm, tn), lambda i, j: (i, j))],
        out_shape=[jax.ShapeDtypeStruct((m, d), F32), jax.ShapeDtypeStruct((m, d), BF16)],
        compiler_params=_params(("parallel", "arbitrary"), blocks + 2 * _nbytes((tm, tn), F32)),
        name="ple",
    )(xb, w_gate, p, w_proj, x)


def _rope_tables(seq):
    half = ROPE_DIM // 2
    pos = jnp.arange(seq, dtype=F32)
    inv = 1.0 / (ROPE_THETA ** (jnp.arange(0, ROPE_DIM, 2, dtype=F32) / ROPE_DIM))
    ang = pos[:, None] * inv[None, :]
    cos, sin = jnp.cos(ang), jnp.sin(ang)
    zeros = jnp.zeros((seq, half), F32)
    pad = jnp.zeros((seq, LANES_V7X - ROPE_DIM), F32)
    cos_t = jnp.concatenate([cos, cos, pad], axis=1)
    sin_a = jnp.concatenate([-sin, zeros, pad], axis=1)
    sin_b = jnp.concatenate([zeros, sin, pad], axis=1)
    return cos_t, sin_a, sin_b


def _t5_bucket(rel):
    nb = REL_BUCKETS // 2
    max_exact = nb // 2
    ret = (rel > 0).astype(jnp.int32) * nb
    n = jnp.abs(rel)
    nf = jnp.maximum(n, 1).astype(F32)
    large = max_exact + (jnp.log(nf / max_exact) / math.log(REL_MAX_DIST / max_exact)
                         * (nb - max_exact)).astype(jnp.int32)
    large = jnp.minimum(large, nb - 1)
    return ret + jnp.where(n < max_exact, n, large)


def _bias_vector(rel_bias, seq):
    rel = jnp.arange(-(seq - 1), seq + 1, dtype=jnp.int32)
    vec = rel_bias.astype(F32)[_t5_bucket(rel)] * LOG2E
    return jnp.transpose(vec)[:, None, :]


def _mla_weights(w_in, w_uq, w_ukv, heads, q_rank, kv_rank):
    d = w_in.shape[0]
    w_cat = jnp.concatenate(
        [w_in, jnp.zeros((d, LANES_V7X - ROPE_DIM), w_in.dtype)], axis=1).astype(BF16)
    wq = w_uq.reshape(q_rank, heads, HEAD_DIM + ROPE_DIM)
    wq = jnp.pad(wq, ((0, 0), (0, 0), (0, 2 * HEAD_DIM - HEAD_DIM - ROPE_DIM)))
    wq = wq.reshape(q_rank, heads * 2 * HEAD_DIM).astype(BF16)
    wkv = w_ukv.reshape(kv_rank, heads, 2 * HEAD_DIM)
    wk = wkv[:, :, :HEAD_DIM].reshape(kv_rank, heads * HEAD_DIM).astype(BF16)
    wv = wkv[:, :, HEAD_DIM:].reshape(kv_rank, heads * HEAD_DIM).astype(BF16)
    return w_cat, wq, wk, wv


def kernel(x, p, mla_w_in, mla_q_norm, mla_kv_norm, mla_w_uq, mla_w_ukv, mla_w_o, diff_w_in, diff_lambda, diff_sub_norm, diff_w_o, rel_bias, ln_g, ln_b, ffn_w_in, ffn_w_out, ple_w_gate, ple_w_proj):
    batch, seq, d = x.shape
    depth = p.shape[0]
    tokens = batch * seq
    alpha = (2 * depth) ** 0.25
    mla_heads = mla_w_o.shape[1] // HEAD_DIM
    diff_heads = diff_w_o.shape[1] // (2 * HEAD_DIM)
    q_rank, kv_rank = mla_q_norm.shape[1], mla_kv_norm.shape[1]
    diff_tq = _tile(seq, 256)

    rope_tabs = _rope_tables(seq)
    bias_master = _bias_master(_bias_vector(rel_bias, seq), diff_tq)

    xf = x.reshape(tokens, d)
    xb = xf.astype(BF16)
    for i in range(depth):
        j = i // 2
        if i % 2 == 0:
            w_cat, wq, wk, wv = _mla_weights(mla_w_in[j], mla_w_uq[j], mla_w_ukv[j],
                                             mla_heads, q_rank, kv_rank)
            cq, ckv, kr = _mla_in(xb, w_cat, mla_q_norm[j][None], mla_kv_norm[j][None],
                                  rope_tabs, seq)
            q = _mla_q(cq, wq, rope_tabs, seq)
            k, v = _mla_kv(ckv, wk, wv, kr)
            o = _mla_attn(q, k, v, batch, seq, mla_heads)
            w_o = mla_w_o[j]
        else:
            qkv = _matmul(xb, diff_w_in[j].astype(BF16), BF16)
            o = _diff_attn(qkv, diff_lambda[j], diff_sub_norm[j][None], bias_master,
                           batch, seq, diff_heads, i, diff_tq)
            w_o = diff_w_o[j]
        xf, xb = _proj_ln(o, w_o.astype(BF16), xf, ln_g[i, 0][None], ln_b[i, 0][None], alpha)
        h = _ffn_up(xb, ffn_w_in[i].astype(BF16))
        xf, xb = _ffn_down_ln(h, ffn_w_out[i].astype(BF16), xf, ln_g[i, 1][None],
                              ln_b[i, 1][None], alpha)
        xf, xb = _ple(xb, xf, p[i].reshape(tokens, -1), ple_w_gate[i].astype(BF16),
                      ple_w_proj[i].astype(BF16))
    return xf.reshape(batch, seq, d)
```

```python
import functools
import math

import jax
import jax.numpy as jnp
from jax import lax
from jax.experimental import pallas as pl
from jax.experimental.pallas import tpu as pltpu

LANES_V7X = 128
VMEM_BYTES_V7X = 64 * 1024 * 1024
VMEM_LIMIT_CAP = VMEM_BYTES_V7X - 8 * 1024 * 1024

HEAD_DIM = 128
ROPE_DIM = 64
ROPE_THETA = 10000.0
REL_BUCKETS = 32
REL_MAX_DIST = 128
LN_EPS = 1e-5
RMS_EPS = 1e-6
LOG2E = math.log2(math.e)

BF16 = jnp.bfloat16
F32 = jnp.float32


def _vmem_limit(block_bytes):
    return int(min(VMEM_LIMIT_CAP, max(32 * 1024 * 1024, 2 * block_bytes + 16 * 1024 * 1024)))


def _nbytes(shape, dtype):
    return math.prod(shape) * jnp.dtype(dtype).itemsize


def _params(semantics, block_bytes):
    return pltpu.CompilerParams(dimension_semantics=semantics,
                                vmem_limit_bytes=_vmem_limit(block_bytes))


def _tile(n, want):
    if n <= want:
        return n
    t = want - want % LANES_V7X
    while t > LANES_V7X and n % t:
        t -= LANES_V7X
    assert n % t == 0, (n, want)
    return t


def _dot(a, b):
    return jnp.dot(a, b, preferred_element_type=F32)


def _dot_nt(a, b):
    return lax.dot_general(a, b, (((1,), (1,)), ((), ())), preferred_element_type=F32)


def _layer_norm(y, g, b):
    mu = jnp.mean(y, axis=-1, keepdims=True)
    yc = y - mu
    var = jnp.mean(yc * yc, axis=-1, keepdims=True)
    return yc * lax.rsqrt(var + LN_EPS) * g + b


def _rms_norm(y, g):
    return y * lax.rsqrt(jnp.mean(y * y, axis=-1, keepdims=True) + RMS_EPS) * g


def _rope_half(r, cos_t, sin_a, sin_b):
    up = pltpu.roll(r, LANES_V7X - ROPE_DIM // 2, axis=1)
    down = pltpu.roll(r, ROPE_DIM // 2, axis=1)
    return r * cos_t + up * sin_a + down * sin_b


def _matmul_kernel(x_ref, w_ref, cs_ref, o_ref):
    o_ref[...] = (_dot(x_ref[...], w_ref[...]) * cs_ref[...]).astype(o_ref.dtype)


def _matmul(x, w, col_scale, out_dtype, tm=1024, tn=1024):
    m, k = x.shape
    n = w.shape[1]
    tm, tn = _tile(m, tm), _tile(n, tn)
    blocks = _nbytes((tm, k), x.dtype) + _nbytes((k, tn), w.dtype) + _nbytes((tm, tn), out_dtype)
    return pl.pallas_call(
        _matmul_kernel,
        grid=(m // tm, n // tn),
        in_specs=[pl.BlockSpec((tm, k), lambda i, j: (i, 0)),
                  pl.BlockSpec((k, tn), lambda i, j: (0, j)),
                  pl.BlockSpec((1, tn), lambda i, j: (0, j))],
        out_specs=pl.BlockSpec((tm, tn), lambda i, j: (i, j)),
        out_shape=jax.ShapeDtypeStruct((m, n), out_dtype),
        compiler_params=_params(("parallel", "arbitrary"), blocks + _nbytes((tm, tn), F32)),
        name="matmul",
    )(x, w, col_scale)


def _mla_in_kernel(x_ref, w_ref, gq_ref, gkv_ref, cos_ref, sa_ref, sb_ref,
                   cq_ref, ckv_ref, kr_ref, *, q_rank, kv_rank):
    h = _dot(x_ref[...], w_ref[...])
    cq_ref[...] = _rms_norm(h[:, :q_rank], gq_ref[...]).astype(cq_ref.dtype)
    ckv_ref[...] = _rms_norm(h[:, q_rank:q_rank + kv_rank], gkv_ref[...]).astype(ckv_ref.dtype)
    kr = h[:, q_rank + kv_rank:]
    kr_ref[...] = _rope_half(kr, cos_ref[...], sa_ref[...], sb_ref[...]).astype(kr_ref.dtype)


def _mla_in(xb, w_cat, gq, gkv, rope_tabs, seq, tm=1024):
    m, k = xb.shape
    q_rank, kv_rank = gq.shape[1], gkv.shape[1]
    n = w_cat.shape[1]
    tm = _tile(seq, tm)
    per_seq = seq // tm
    blocks = (_nbytes((tm, k), BF16) + _nbytes((k, n), BF16) + 3 * _nbytes((tm, LANES_V7X), F32)
              + _nbytes((tm, n), BF16))
    tab_spec = pl.BlockSpec((tm, LANES_V7X), lambda i: (i % per_seq, 0))
    return pl.pallas_call(
        functools.partial(_mla_in_kernel, q_rank=q_rank, kv_rank=kv_rank),
        grid=(m // tm,),
        in_specs=[pl.BlockSpec((tm, k), lambda i: (i, 0)),
                  pl.BlockSpec((k, n), lambda i: (0, 0)),
                  pl.BlockSpec((1, q_rank), lambda i: (0, 0)),
                  pl.BlockSpec((1, kv_rank), lambda i: (0, 0)),
                  tab_spec, tab_spec, tab_spec],
        out_specs=[pl.BlockSpec((tm, q_rank), lambda i: (i, 0)),
                   pl.BlockSpec((tm, kv_rank), lambda i: (i, 0)),
                   pl.BlockSpec((tm, LANES_V7X), lambda i: (i, 0))],
        out_shape=[jax.ShapeDtypeStruct((m, q_rank), BF16),
                   jax.ShapeDtypeStruct((m, kv_rank), BF16),
                   jax.ShapeDtypeStruct((m, LANES_V7X), BF16)],
        compiler_params=_params(("parallel",), blocks + _nbytes((tm, n), F32)),
        name="mla_in",
    )(xb, w_cat, gq, gkv, *rope_tabs)


def _mla_q_kernel(c_ref, w_ref, cos_ref, sa_ref, sb_ref, q_ref, *, heads_per_step, scale):
    acc = _dot(c_ref[...], w_ref[...])
    cos_t, sa, sb = cos_ref[...], sa_ref[...], sb_ref[...]
    for h in range(heads_per_step):
        base = 2 * HEAD_DIM * h
        q_ref[:, base:base + HEAD_DIM] = (acc[:, base:base + HEAD_DIM] * scale).astype(q_ref.dtype)
        rope = _rope_half(acc[:, base + HEAD_DIM:base + 2 * HEAD_DIM], cos_t, sa, sb)
        q_ref[:, base + HEAD_DIM:base + 2 * HEAD_DIM] = rope.astype(q_ref.dtype)


def _mla_q(cq, w_uq, rope_tabs, seq, scale, tm=1024, heads_per_step=4):
    m, k = cq.shape
    n = w_uq.shape[1]
    tm = _tile(seq, tm)
    per_seq = seq // tm
    tn = min(n, heads_per_step * 2 * HEAD_DIM)
    heads_per_step = tn // (2 * HEAD_DIM)
    blocks = (_nbytes((tm, k), BF16) + _nbytes((k, tn), BF16) + 3 * _nbytes((tm, LANES_V7X), F32)
              + _nbytes((tm, tn), BF16))
    tab_spec = pl.BlockSpec((tm, LANES_V7X), lambda i, j: (i % per_seq, 0))
    return pl.pallas_call(
        functools.partial(_mla_q_kernel, heads_per_step=heads_per_step, scale=scale),
        grid=(m // tm, n // tn),
        in_specs=[pl.BlockSpec((tm, k), lambda i, j: (i, 0)),
                  pl.BlockSpec((k, tn), lambda i, j: (0, j)),
                  tab_spec, tab_spec, tab_spec],
        out_specs=pl.BlockSpec((tm, tn), lambda i, j: (i, j)),
        out_shape=jax.ShapeDtypeStruct((m, n), BF16),
        compiler_params=_params(("parallel", "arbitrary"), blocks + _nbytes((tm, tn), F32)),
        name="mla_q",
    )(cq, w_uq, *rope_tabs)


def _mla_kv_kernel(c_ref, wk_ref, wv_ref, kr_ref, k_ref, v_ref, *, heads_per_step):
    c = c_ref[...]
    kn = _dot(c, wk_ref[...])
    v_ref[...] = _dot(c, wv_ref[...]).astype(v_ref.dtype)
    kr = kr_ref[...]
    for h in range(heads_per_step):
        base = 2 * HEAD_DIM * h
        k_ref[:, base:base + HEAD_DIM] = kn[:, HEAD_DIM * h:HEAD_DIM * (h + 1)].astype(k_ref.dtype)
        k_ref[:, base + HEAD_DIM:base + 2 * HEAD_DIM] = kr


def _mla_kv(ckv, w_k, w_v, kr, tm=1024, heads_per_step=4):
    m, k = ckv.shape
    n = w_k.shape[1]
    tm = _tile(m, tm)
    tn = min(n, heads_per_step * HEAD_DIM)
    heads_per_step = tn // HEAD_DIM
    blocks = (_nbytes((tm, k), BF16) + 2 * _nbytes((k, tn), BF16) + _nbytes((tm, LANES_V7X), BF16)
              + 3 * _nbytes((tm, tn), BF16))
    return pl.pallas_call(
        functools.partial(_mla_kv_kernel, heads_per_step=heads_per_step),
        grid=(m // tm, n // tn),
        in_specs=[pl.BlockSpec((tm, k), lambda i, j: (i, 0)),
                  pl.BlockSpec((k, tn), lambda i, j: (0, j)),
                  pl.BlockSpec((k, tn), lambda i, j: (0, j)),
                  pl.BlockSpec((tm, LANES_V7X), lambda i, j: (i, 0))],
        out_specs=[pl.BlockSpec((tm, 2 * tn), lambda i, j: (i, j)),
                   pl.BlockSpec((tm, tn), lambda i, j: (i, j))],
        out_shape=[jax.ShapeDtypeStruct((m, 2 * n), BF16),
                   jax.ShapeDtypeStruct((m, n), BF16)],
        compiler_params=_params(("parallel", "arbitrary"), blocks + 2 * _nbytes((tm, tn), F32)),
        name="mla_kv",
    )(ckv, w_k, w_v, kr)


def _online_softmax_pv(q, k_ref, v_ref, k_cols, tk, bias_fn=None):
    tq = q.shape[0]
    seq = k_ref.shape[0]
    groups = tk // LANES_V7X
    m = jnp.full((tq, 1), -jnp.inf, F32)
    l_acc = jnp.zeros((tq, LANES_V7X), F32)
    acc = jnp.zeros((tq, v_ref.shape[1]), F32)
    for c in range(seq // tk):
        s = _dot_nt(q, k_ref[c * tk:(c + 1) * tk, k_cols])
        if bias_fn is not None:
            s = s + bias_fn(c)
        m_lane = s[:, :LANES_V7X]
        for g in range(1, groups):
            m_lane = jnp.maximum(m_lane, s[:, g * LANES_V7X:(g + 1) * LANES_V7X])
        m_new = jnp.maximum(m, jnp.max(m_lane, axis=-1, keepdims=True))
        alpha = jnp.exp2(m - m_new)
        p = jnp.exp2(s - m_new)
        p_sum = p[:, :LANES_V7X]
        for g in range(1, groups):
            p_sum = p_sum + p[:, g * LANES_V7X:(g + 1) * LANES_V7X]
        l_acc = alpha * l_acc + p_sum
        acc = alpha * acc + _dot(p.astype(BF16), v_ref[c * tk:(c + 1) * tk, :])
        m = m_new
    return acc, jnp.sum(l_acc, axis=-1, keepdims=True)


def _mla_attn_kernel(q_ref, k_ref, v_ref, o_ref, *, tk):
    acc, l = _online_softmax_pv(q_ref[...], k_ref, v_ref, slice(None), tk)
    o_ref[...] = (acc / l).astype(o_ref.dtype)


def _mla_attn(q, k, v, batch, seq, heads, tq=1024, tk=256):
    tq, tk = _tile(seq, tq), _tile(seq, tk)
    nq = seq // tq
    dk, dv = 2 * HEAD_DIM, HEAD_DIM
    blocks = (_nbytes((tq, dk), BF16) + _nbytes((seq, dk), BF16) + _nbytes((seq, dv), BF16)
              + _nbytes((tq, dv), BF16))
    return pl.pallas_call(
        functools.partial(_mla_attn_kernel, tk=tk),
        grid=(batch, heads, nq),
        in_specs=[pl.BlockSpec((tq, dk), lambda b, h, i: (b * nq + i, h)),
                  pl.BlockSpec((seq, dk), lambda b, h, i: (b, h)),
                  pl.BlockSpec((seq, dv), lambda b, h, i: (b, h))],
        out_specs=pl.BlockSpec((tq, dv), lambda b, h, i: (b * nq + i, h)),
        out_shape=jax.ShapeDtypeStruct((batch * seq, heads * dv), BF16),
        compiler_params=_params(("parallel", "parallel", "arbitrary"),
                                blocks + 4 * _nbytes((tq, tk), F32)),
        name="mla_attn",
    )(q, k, v)


def _bias_master_kernel(vec_ref, o_ref, *, tq):
    width = vec_ref.shape[-1]
    m = jnp.broadcast_to(vec_ref[0], (tq, width))
    left = (tq - 1) - lax.broadcasted_iota(jnp.int32, (tq, 1), 0)
    step = 1
    while step < tq:
        shifted = pltpu.roll(m, width - step, axis=1)
        m = jnp.where((left & step) != 0, shifted, m)
        step *= 2
    o_ref[0] = m


def _bias_master(vec, tq):
    heads, _, width = vec.shape
    blocks = _nbytes((1, width), F32) + _nbytes((tq, width), F32)
    return pl.pallas_call(
        functools.partial(_bias_master_kernel, tq=tq),
        grid=(heads,),
        in_specs=[pl.BlockSpec((1, 1, width), lambda h: (h, 0, 0))],
        out_specs=pl.BlockSpec((1, tq, width), lambda h: (h, 0, 0)),
        out_shape=jax.ShapeDtypeStruct((heads, tq, width), F32),
        compiler_params=_params(("parallel",), blocks + 2 * _nbytes((tq, width), F32)),
        name="bias_master",
    )(vec)


def _diff_attn_kernel(lam_ref, g_ref, q_ref, k_ref, v_ref, bias_ref, o_ref,
                      *, tk, lambda_init):
    tq = q_ref.shape[0]
    seq = k_ref.shape[0]
    sub = bias_ref.shape[1]
    lam = lam_ref[...]
    lam_full = (jnp.exp(jnp.sum(lam[0:1] * lam[1:2], axis=-1, keepdims=True))
                - jnp.exp(jnp.sum(lam[2:3] * lam[3:4], axis=-1, keepdims=True)) + lambda_init)
    row0 = pl.program_id(2) * tq

    def bias_fn(c):
        parts = []
        for j in range(tq // sub):
            off = pl.multiple_of(seq - sub - row0 - j * sub + c * tk, LANES_V7X)
            parts.append(bias_ref[0, :, pl.ds(off, tk)])
        return parts[0] if len(parts) == 1 else jnp.concatenate(parts, axis=0)

    outs = []
    for g in range(2):
        cols = slice(g * HEAD_DIM, (g + 1) * HEAD_DIM)
        acc, l = _online_softmax_pv(q_ref[:, cols], k_ref, v_ref, cols, tk, bias_fn)
        outs.append(acc / l)
    o = outs[0] - lam_full * outs[1]
    o = _rms_norm(o, g_ref[...]) * (1.0 - lambda_init)
    o_ref[...] = o.astype(o_ref.dtype)


def _diff_attn(qkv, lam, sub_norm, bias_master, batch, seq, heads, layer_idx, tq=512, tk=512):
    sub = bias_master.shape[1]
    tq, tk = _tile(seq, tq), _tile(seq, tk)
    assert tq % sub == 0 and sub % LANES_V7X == 0, (tq, sub)
    nq = seq // tq
    d2 = 2 * HEAD_DIM
    lambda_init = 0.8 - 0.6 * math.exp(-0.3 * layer_idx)
    blocks = (2 * _nbytes((tq, d2), BF16) + 2 * _nbytes((seq, d2), BF16)
              + _nbytes((sub, 2 * seq), F32))
    return pl.pallas_call(
        functools.partial(_diff_attn_kernel, tk=tk, lambda_init=lambda_init),
        grid=(heads, batch, nq),
        in_specs=[pl.BlockSpec(lam.shape, lambda h, b, i: (0, 0)),
                  pl.BlockSpec((1, d2), lambda h, b, i: (0, 0)),
                  pl.BlockSpec((tq, d2), lambda h, b, i: (b * nq + i, h)),
                  pl.BlockSpec((seq, d2), lambda h, b, i: (b, heads + h)),
                  pl.BlockSpec((seq, d2), lambda h, b, i: (b, 2 * heads + h)),
                  pl.BlockSpec((1, sub, 2 * seq), lambda h, b, i: (h, 0, 0))],
        out_specs=pl.BlockSpec((tq, d2), lambda h, b, i: (b * nq + i, h)),
        out_shape=jax.ShapeDtypeStruct((batch * seq, heads * d2), BF16),
        compiler_params=_params(("parallel", "parallel", "arbitrary"),
                                blocks + 4 * _nbytes((tq, tk), F32)),
        name="diff_attn",
    )(lam, sub_norm, qkv, qkv, qkv, bias_master)


def _proj_ln_kernel(a_ref, w_ref, x_ref, g_ref, b_ref, y_ref, yb_ref, *, alpha):
    y = alpha * x_ref[...] + _dot(a_ref[...], w_ref[...])
    y = _layer_norm(y, g_ref[...], b_ref[...])
    y_ref[...] = y
    yb_ref[...] = y.astype(yb_ref.dtype)


def _proj_ln(a, w, x, g, b, alpha, tm=512):
    m, k = a.shape
    d = w.shape[1]
    tm = _tile(m, tm)
    blocks = (_nbytes((tm, k), BF16) + _nbytes((k, d), BF16) + 2 * _nbytes((tm, d), F32)
              + _nbytes((tm, d), BF16))
    row = lambda i: (i, 0)
    fixed = lambda i: (0, 0)
    return pl.pallas_call(
        functools.partial(_proj_ln_kernel, alpha=alpha),
        grid=(m // tm,),
        in_specs=[pl.BlockSpec((tm, k), row), pl.BlockSpec((k, d), fixed),
                  pl.BlockSpec((tm, d), row), pl.BlockSpec((1, d), fixed),
                  pl.BlockSpec((1, d), fixed)],
        out_specs=[pl.BlockSpec((tm, d), row), pl.BlockSpec((tm, d), row)],
        out_shape=[jax.ShapeDtypeStruct((m, d), F32), jax.ShapeDtypeStruct((m, d), BF16)],
        compiler_params=_params(("parallel",), blocks + 2 * _nbytes((tm, d), F32)),
        name="proj_ln",
    )(a, w, x, g, b)


def _ffn_up_kernel(x_ref, wg_ref, wu_ref, h_ref):
    x = x_ref[...]
    g = _dot(x, wg_ref[...])
    u = _dot(x, wu_ref[...])
    h_ref[...] = (g * jax.nn.sigmoid(g) * u).astype(h_ref.dtype)


def _ffn_up(xb, w_in, tm=1024, tn=512):
    m, k = xb.shape
    ff = w_in.shape[1] // 2
    tm, tn = _tile(m, tm), _tile(ff, tn)
    n_tiles = ff // tn
    blocks = _nbytes((tm, k), BF16) + 2 * _nbytes((k, tn), BF16) + _nbytes((tm, tn), BF16)
    return pl.pallas_call(
        _ffn_up_kernel,
        grid=(m // tm, n_tiles),
        in_specs=[pl.BlockSpec((tm, k), lambda i, j: (i, 0)),
                  pl.BlockSpec((k, tn), lambda i, j: (0, j)),
                  pl.BlockSpec((k, tn), lambda i, j: (0, j + n_tiles))],
        out_specs=pl.BlockSpec((tm, tn), lambda i, j: (i, j)),
        out_shape=jax.ShapeDtypeStruct((m, ff), BF16),
        compiler_params=_params(("parallel", "arbitrary"), blocks + 3 * _nbytes((tm, tn), F32)),
        name="ffn_up",
    )(xb, w_in, w_in)


def _ffn_down_ln_kernel(h_ref, w_ref, x_ref, g_ref, b_ref, y_ref, yb_ref, *, alpha):
    kk = pl.program_id(1)

    @pl.when(kk == 0)
    def _():
        y_ref[...] = alpha * x_ref[...]

    y_ref[...] += _dot(h_ref[...], w_ref[...])

    @pl.when(kk == pl.num_programs(1) - 1)
    def _():
        y = _layer_norm(y_ref[...], g_ref[...], b_ref[...])
        y_ref[...] = y
        yb_ref[...] = y.astype(yb_ref.dtype)


def _ffn_down_ln(h, w, x, g, b, alpha, tm=512, tk=1408):
    m, ff = h.shape
    d = w.shape[1]
    tm, tk = _tile(m, tm), _tile(ff, tk)
    blocks = (_nbytes((tm, tk), BF16) + _nbytes((tk, d), BF16) + 2 * _nbytes((tm, d), F32)
              + _nbytes((tm, d), BF16))
    row = lambda i, kk: (i, 0)
    fixed = lambda i, kk: (0, 0)
    return pl.pallas_call(
        functools.partial(_ffn_down_ln_kernel, alpha=alpha),
        grid=(m // tm, ff // tk),
        in_specs=[pl.BlockSpec((tm, tk), lambda i, kk: (i, kk)),
                  pl.BlockSpec((tk, d), lambda i, kk: (kk, 0)),
                  pl.BlockSpec((tm, d), row), pl.BlockSpec((1, d), fixed),
                  pl.BlockSpec((1, d), fixed)],
        out_specs=[pl.BlockSpec((tm, d), row), pl.BlockSpec((tm, d), row)],
        out_shape=[jax.ShapeDtypeStruct((m, d), F32), jax.ShapeDtypeStruct((m, d), BF16)],
        compiler_params=_params(("parallel", "arbitrary"), blocks + _nbytes((tm, d), F32)),
        name="ffn_down_ln",
    )(h, w, x, g, b)


def _ple_kernel(xb_ref, wg_ref, p_ref, wp_ref, x_ref, y_ref, yb_ref):
    gate = jax.nn.sigmoid(_dot(xb_ref[...], wg_ref[...]))
    proj = _dot(p_ref[...].astype(BF16), wp_ref[...])
    y = x_ref[...] + gate * proj
    y_ref[...] = y
    yb_ref[...] = y.astype(yb_ref.dtype)


def _ple(xb, x, p, w_gate, w_proj, tm=1024, tn=1024):
    m, d = x.shape
    pd = p.shape[1]
    tm, tn = _tile(m, tm), _tile(d, tn)
    blocks = (_nbytes((tm, d), BF16) + _nbytes((d, tn), BF16) + _nbytes((tm, pd), F32)
              + _nbytes((pd, tn), BF16) + 2 * _nbytes((tm, tn), F32) + _nbytes((tm, tn), BF16))
    return pl.pallas_call(
        _ple_kernel,
        grid=(m // tm, d // tn),
        in_specs=[pl.BlockSpec((tm, d), lambda i, j: (i, 0)),
                  pl.BlockSpec((d, tn), lambda i, j: (0, j)),
                  pl.BlockSpec((tm, pd), lambda i, j: (i, 0)),
                  pl.BlockSpec((pd, tn), lambda i, j: (0, j)),
                  pl.BlockSpec((tm, tn), lambda i, j: (i, j))],
        out_specs=[pl.BlockSpec((tm, tn), lambda i, j: (i, j)),
                   pl.BlockSpec((tm, tn), lambda i, j: (i, j))],
        out_shape=[jax.ShapeDtypeStruct((m, d), F32), jax.ShapeDtypeStruct((m, d), BF16)],
        compiler_params=_params(("parallel", "arbitrary"), blocks + 2 * _nbytes((tm, tn), F32)),
        name="ple",
    )(xb, w_gate, p, w_proj, x)


def _rope_tables(seq):
    half = ROPE_DIM // 2
    pos = jnp.arange(seq, dtype=F32)
    inv = 1.0 / (ROPE_THETA ** (jnp.arange(0, ROPE_DIM, 2, dtype=F32) / ROPE_DIM))
    ang = pos[:, None] * inv[None, :]
    cos, sin = jnp.cos(ang), jnp.sin(ang)
    zeros = jnp.zeros((seq, half), F32)
    pad = jnp.zeros((seq, LANES_V7X - ROPE_DIM), F32)
    cos_t = jnp.concatenate([cos, cos, pad], axis=1)
    sin_a = jnp.concatenate([-sin, zeros, pad], axis=1)
    sin_b = jnp.concatenate([zeros, sin, pad], axis=1)
    return cos_t, sin_a, sin_b


def _t5_bucket(rel):
    nb = REL_BUCKETS // 2
    max_exact = nb // 2
    ret = (rel > 0).astype(jnp.int32) * nb
    n = jnp.abs(rel)
    nf = jnp.maximum(n, 1).astype(F32)
    large = max_exact + (jnp.log(nf / max_exact) / math.log(REL_MAX_DIST / max_exact)
                         * (nb - max_exact)).astype(jnp.int32)
    large = jnp.minimum(large, nb - 1)
    return ret + jnp.where(n < max_exact, n, large)


def _bias_vector(rel_bias, seq):
    rel = jnp.arange(-(seq - 1), seq + 1, dtype=jnp.int32)
    vec = rel_bias.astype(F32)[_t5_bucket(rel)] * LOG2E
    return jnp.transpose(vec)[:, None, :]


def _mla_weights(w_in, w_uq, w_ukv, heads, q_rank, kv_rank):
    d = w_in.shape[0]
    w_cat = jnp.concatenate(
        [w_in, jnp.zeros((d, LANES_V7X - ROPE_DIM), w_in.dtype)], axis=1).astype(BF16)
    wq = w_uq.reshape(q_rank, heads, HEAD_DIM + ROPE_DIM)
    wq = jnp.pad(wq, ((0, 0), (0, 0), (0, 2 * HEAD_DIM - HEAD_DIM - ROPE_DIM)))
    wq = wq.reshape(q_rank, heads * 2 * HEAD_DIM).astype(BF16)
    wkv = w_ukv.reshape(kv_rank, heads, 2 * HEAD_DIM)
    wk = wkv[:, :, :HEAD_DIM].reshape(kv_rank, heads * HEAD_DIM).astype(BF16)
    wv = wkv[:, :, HEAD_DIM:].reshape(kv_rank, heads * HEAD_DIM).astype(BF16)
    return w_cat, wq, wk, wv


def kernel(x, p, mla_w_in, mla_q_norm, mla_kv_norm, mla_w_uq, mla_w_ukv, mla_w_o, diff_w_in, diff_lambda, diff_sub_norm, diff_w_o, rel_bias, ln_g, ln_b, ffn_w_in, ffn_w_out, ple_w_gate, ple_w_proj):
    batch, seq, d = x.shape
    depth = p.shape[0]
    tokens = batch * seq
    alpha = (2 * depth) ** 0.25
    mla_heads = mla_w_o.shape[1] // HEAD_DIM
    diff_heads = diff_w_o.shape[1] // (2 * HEAD_DIM)
    q_rank, kv_rank = mla_q_norm.shape[1], mla_kv_norm.shape[1]
    mla_scale = (HEAD_DIM + ROPE_DIM) ** -0.5 * LOG2E
    diff_scale = HEAD_DIM ** -0.5 * LOG2E
    diff_q_cols = diff_heads * 2 * HEAD_DIM
    diff_col_scale = jnp.concatenate(
        [jnp.full((1, diff_q_cols), diff_scale, F32),
         jnp.ones((1, diff_w_in.shape[2] - diff_q_cols), F32)], axis=1)

    rope_tabs = _rope_tables(seq)
    rope_tabs_q = tuple(t * mla_scale for t in rope_tabs)
    bias_master = _bias_master(_bias_vector(rel_bias, seq), _tile(seq, 256))

    xf = x.reshape(tokens, d)
    xb = xf.astype(BF16)
    for i in range(depth):
        j = i // 2
        if i % 2 == 0:
            w_cat, wq, wk, wv = _mla_weights(mla_w_in[j], mla_w_uq[j], mla_w_ukv[j],
                                             mla_heads, q_rank, kv_rank)
            cq, ckv, kr = _mla_in(xb, w_cat, mla_q_norm[j][None], mla_kv_norm[j][None],
                                  rope_tabs, seq)
            q = _mla_q(cq, wq, rope_tabs_q, seq, mla_scale)
            k, v = _mla_kv(ckv, wk, wv, kr)
            o = _mla_attn(q, k, v, batch, seq, mla_heads)
            w_o = mla_w_o[j]
        else:
            qkv = _matmul(xb, diff_w_in[j].astype(BF16), diff_col_scale, BF16)
            o = _diff_attn(qkv, diff_lambda[j], diff_sub_norm[j][None], bias_master,
                           batch, seq, diff_heads, i)
            w_o = diff_w_o[j]
        xf, xb = _proj_ln(o, w_o.astype(BF16), xf, ln_g[i, 0][None], ln_b[i, 0][None], alpha)
        h = _ffn_up(xb, ffn_w_in[i].astype(BF16))
        xf, xb = _ffn_down_ln(h, ffn_w_out[i].astype(BF16), xf, ln_g[i, 1][None],
                              ln_b[i, 1][None], alpha)
        xf, xb = _ple(xb, xf, p[i].reshape(tokens, -1), ple_w_gate[i].astype(BF16),
                      ple_w_proj[i].astype(BF16))
    return xf.reshape(batch, seq, d)
```

```python
import functools
import math

import jax
import jax.numpy as jnp
from jax import lax
from jax.experimental import pallas as pl
from jax.experimental.pallas import tpu as pltpu

LANES_V7X = 128
VMEM_BYTES_V7X = 64 * 1024 * 1024
VMEM_LIMIT_CAP = VMEM_BYTES_V7X - 8 * 1024 * 1024

HEAD_DIM = 128
ROPE_DIM = 64
ROPE_THETA = 10000.0
REL_BUCKETS = 32
REL_MAX_DIST = 128
LN_EPS = 1e-5
RMS_EPS = 1e-6
LOG2E = math.log2(math.e)

BF16 = jnp.bfloat16
F32 = jnp.float32


def _vmem_limit(block_bytes):
    return int(min(VMEM_LIMIT_CAP, max(32 * 1024 * 1024, 2 * block_bytes + 16 * 1024 * 1024)))


def _nbytes(shape, dtype):
    return math.prod(shape) * jnp.dtype(dtype).itemsize


def _params(semantics, block_bytes):
    return pltpu.CompilerParams(dimension_semantics=semantics,
                                vmem_limit_bytes=_vmem_limit(block_bytes))


def _tile(n, want):
    if n <= want:
        return n
    t = want - want % LANES_V7X
    while t > LANES_V7X and n % t:
        t -= LANES_V7X
    assert n % t == 0, (n, want)
    return t


def _layer_spec(block, index_map, layer):
    return pl.BlockSpec((None,) + tuple(block), lambda *g: (layer,) + tuple(index_map(*g)))


def _dot(a, b):
    return jnp.dot(a, b, preferred_element_type=F32)


def _dot_nt(a, b):
    return lax.dot_general(a, b, (((1,), (1,)), ((), ())), preferred_element_type=F32)


def _layer_norm(y, g, b):
    mu = jnp.mean(y, axis=-1, keepdims=True)
    yc = y - mu
    var = jnp.mean(yc * yc, axis=-1, keepdims=True)
    return yc * lax.rsqrt(var + LN_EPS) * g + b


def _rms_norm(y, g):
    return y * lax.rsqrt(jnp.mean(y * y, axis=-1, keepdims=True) + RMS_EPS) * g


def _rope_half(r, cos_t, sin_a, sin_b):
    up = pltpu.roll(r, LANES_V7X - ROPE_DIM // 2, axis=1)
    down = pltpu.roll(r, ROPE_DIM // 2, axis=1)
    return r * cos_t + up * sin_a + down * sin_b


def _matmul_kernel(x_ref, w_ref, cs_ref, o_ref):
    o_ref[...] = (_dot(x_ref[...], w_ref[...]) * cs_ref[...]).astype(o_ref.dtype)


def _matmul(x, w, layer, col_scale, out_dtype, tm=1024, tn=1024):
    m, k = x.shape
    n = w.shape[2]
    tm, tn = _tile(m, tm), _tile(n, tn)
    blocks = _nbytes((tm, k), x.dtype) + _nbytes((k, tn), w.dtype) + _nbytes((tm, tn), out_dtype)
    return pl.pallas_call(
        _matmul_kernel,
        grid=(m // tm, n // tn),
        in_specs=[pl.BlockSpec((tm, k), lambda i, j: (i, 0)),
                  _layer_spec((k, tn), lambda i, j: (0, j), layer),
                  pl.BlockSpec((1, tn), lambda i, j: (0, j))],
        out_specs=pl.BlockSpec((tm, tn), lambda i, j: (i, j)),
        out_shape=jax.ShapeDtypeStruct((m, n), out_dtype),
        compiler_params=_params(("parallel", "arbitrary"), blocks + _nbytes((tm, tn), F32)),
        name="matmul",
    )(x, w, col_scale)


def _mla_in_kernel(x_ref, w_ref, gq_ref, gkv_ref, cos_ref, sa_ref, sb_ref,
                   cq_ref, ckv_ref, kr_ref, *, q_rank, kv_rank, sub):
    gq, gkv = gq_ref[...], gkv_ref[...]
    for r in range(x_ref.shape[0] // sub):
        rows = slice(r * sub, (r + 1) * sub)
        h = _dot(x_ref[rows, :], w_ref[...])
        cq_ref[rows, :] = _rms_norm(h[:, :q_rank], gq).astype(cq_ref.dtype)
        ckv_ref[rows, :] = _rms_norm(h[:, q_rank:q_rank + kv_rank], gkv).astype(ckv_ref.dtype)
        kr = _rope_half(h[:, q_rank + kv_rank:], cos_ref[rows, :], sa_ref[rows, :], sb_ref[rows, :])
        kr_ref[rows, :] = kr.astype(kr_ref.dtype)


def _mla_in(xb, w_cat, gq, gkv, rope_tabs, seq, tm=1024, sub=256):
    m, k = xb.shape
    q_rank, kv_rank = gq.shape[1], gkv.shape[1]
    n = w_cat.shape[1]
    tm = _tile(seq, tm)
    sub = math.gcd(tm, sub)
    per_seq = seq // tm
    blocks = (_nbytes((tm, k), BF16) + _nbytes((k, n), BF16) + 3 * _nbytes((tm, LANES_V7X), F32)
              + _nbytes((tm, n), BF16))
    tab_spec = pl.BlockSpec((tm, LANES_V7X), lambda i: (i % per_seq, 0))
    return pl.pallas_call(
        functools.partial(_mla_in_kernel, q_rank=q_rank, kv_rank=kv_rank, sub=sub),
        grid=(m // tm,),
        in_specs=[pl.BlockSpec((tm, k), lambda i: (i, 0)),
                  pl.BlockSpec((k, n), lambda i: (0, 0)),
                  pl.BlockSpec((1, q_rank), lambda i: (0, 0)),
                  pl.BlockSpec((1, kv_rank), lambda i: (0, 0)),
                  tab_spec, tab_spec, tab_spec],
        out_specs=[pl.BlockSpec((tm, q_rank), lambda i: (i, 0)),
                   pl.BlockSpec((tm, kv_rank), lambda i: (i, 0)),
                   pl.BlockSpec((tm, LANES_V7X), lambda i: (i, 0))],
        out_shape=[jax.ShapeDtypeStruct((m, q_rank), BF16),
                   jax.ShapeDtypeStruct((m, kv_rank), BF16),
                   jax.ShapeDtypeStruct((m, LANES_V7X), BF16)],
        compiler_params=_params(("parallel",), blocks + _nbytes((tm, n), F32)),
        name="mla_in",
    )(xb, w_cat, gq, gkv, *rope_tabs)


def _mla_q_kernel(c_ref, w_ref, cos_ref, sa_ref, sb_ref, q_ref, *, heads_per_step, scale):
    acc = _dot(c_ref[...], w_ref[...])
    cos_t, sa, sb = cos_ref[...], sa_ref[...], sb_ref[...]
    for h in range(heads_per_step):
        base = 2 * HEAD_DIM * h
        q_ref[:, base:base + HEAD_DIM] = (acc[:, base:base + HEAD_DIM] * scale).astype(q_ref.dtype)
        rope = _rope_half(acc[:, base + HEAD_DIM:base + 2 * HEAD_DIM], cos_t, sa, sb)
        q_ref[:, base + HEAD_DIM:base + 2 * HEAD_DIM] = rope.astype(q_ref.dtype)


def _mla_q(cq, w_uq, rope_tabs, seq, scale, tm=1024, heads_per_step=8):
    m, k = cq.shape
    n = w_uq.shape[1]
    tm = _tile(seq, tm)
    per_seq = seq // tm
    tn = min(n, heads_per_step * 2 * HEAD_DIM)
    heads_per_step = tn // (2 * HEAD_DIM)
    blocks = (_nbytes((tm, k), BF16) + _nbytes((k, tn), BF16) + 3 * _nbytes((tm, LANES_V7X), F32)
              + _nbytes((tm, tn), BF16))
    tab_spec = pl.BlockSpec((tm, LANES_V7X), lambda i, j: (i % per_seq, 0))
    return pl.pallas_call(
        functools.partial(_mla_q_kernel, heads_per_step=heads_per_step, scale=scale),
        grid=(m // tm, n // tn),
        in_specs=[pl.BlockSpec((tm, k), lambda i, j: (i, 0)),
                  pl.BlockSpec((k, tn), lambda i, j: (0, j)),
                  tab_spec, tab_spec, tab_spec],
        out_specs=pl.BlockSpec((tm, tn), lambda i, j: (i, j)),
        out_shape=jax.ShapeDtypeStruct((m, n), BF16),
        compiler_params=_params(("parallel", "arbitrary"), blocks + _nbytes((tm, tn), F32)),
        name="mla_q",
    )(cq, w_uq, *rope_tabs)


def _mla_kv_kernel(c_ref, wk_ref, wv_ref, kr_ref, k_ref, v_ref, *, heads_per_step):
    c = c_ref[...]
    kn = _dot(c, wk_ref[...])
    v_ref[...] = _dot(c, wv_ref[...]).astype(v_ref.dtype)
    kr = kr_ref[...]
    for h in range(heads_per_step):
        base = 2 * HEAD_DIM * h
        k_ref[:, base:base + HEAD_DIM] = kn[:, HEAD_DIM * h:HEAD_DIM * (h + 1)].astype(k_ref.dtype)
        k_ref[:, base + HEAD_DIM:base + 2 * HEAD_DIM] = kr


def _mla_kv(ckv, w_k, w_v, kr, tm=1024, heads_per_step=8):
    m, k = ckv.shape
    n = w_k.shape[1]
    tm = _tile(m, tm)
    tn = min(n, heads_per_step * HEAD_DIM)
    heads_per_step = tn // HEAD_DIM
    blocks = (_nbytes((tm, k), BF16) + 2 * _nbytes((k, tn), BF16) + _nbytes((tm, LANES_V7X), BF16)
              + 3 * _nbytes((tm, tn), BF16))
    return pl.pallas_call(
        functools.partial(_mla_kv_kernel, heads_per_step=heads_per_step),
        grid=(m // tm, n // tn),
        in_specs=[pl.BlockSpec((tm, k), lambda i, j: (i, 0)),
                  pl.BlockSpec((k, tn), lambda i, j: (0, j)),
                  pl.BlockSpec((k, tn), lambda i, j: (0, j)),
                  pl.BlockSpec((tm, LANES_V7X), lambda i, j: (i, 0))],
        out_specs=[pl.BlockSpec((tm, 2 * tn), lambda i, j: (i, j)),
                   pl.BlockSpec((tm, tn), lambda i, j: (i, j))],
        out_shape=[jax.ShapeDtypeStruct((m, 2 * n), BF16),
                   jax.ShapeDtypeStruct((m, n), BF16)],
        compiler_params=_params(("parallel", "arbitrary"), blocks + 2 * _nbytes((tm, tn), F32)),
        name="mla_kv",
    )(ckv, w_k, w_v, kr)


def _online_softmax_pv(q, k_ref, v_ref, k_cols, tk, bias_fn=None):
    tq = q.shape[0]
    seq = k_ref.shape[0]
    groups = tk // LANES_V7X
    m = jnp.full((tq, 1), -jnp.inf, F32)
    l_acc = jnp.zeros((tq, LANES_V7X), F32)
    acc = jnp.zeros((tq, v_ref.shape[1]), F32)
    for c in range(seq // tk):
        s = _dot_nt(q, k_ref[c * tk:(c + 1) * tk, k_cols])
        if bias_fn is not None:
            s = s + bias_fn(c)
        m_lane = s[:, :LANES_V7X]
        for g in range(1, groups):
            m_lane = jnp.maximum(m_lane, s[:, g * LANES_V7X:(g + 1) * LANES_V7X])
        m_new = jnp.maximum(m, jnp.max(m_lane, axis=-1, keepdims=True))
        alpha = jnp.exp2(m - m_new)
        p = jnp.exp2(s - m_new)
        p_sum = p[:, :LANES_V7X]
        for g in range(1, groups):
            p_sum = p_sum + p[:, g * LANES_V7X:(g + 1) * LANES_V7X]
        l_acc = alpha * l_acc + p_sum
        acc = alpha * acc + _dot(p.astype(BF16), v_ref[c * tk:(c + 1) * tk, :])
        m = m_new
    return acc, jnp.sum(l_acc, axis=-1, keepdims=True)


def _mla_attn_kernel(q_ref, k_ref, v_ref, o_ref, *, tk):
    acc, l = _online_softmax_pv(q_ref[...], k_ref, v_ref, slice(None), tk)
    o_ref[...] = (acc / l).astype(o_ref.dtype)


def _mla_attn(q, k, v, batch, seq, heads, tq=2048, tk=256):
    tq, tk = _tile(seq, tq), _tile(seq, tk)
    nq = seq // tq
    dk, dv = 2 * HEAD_DIM, HEAD_DIM
    blocks = (_nbytes((tq, dk), BF16) + _nbytes((seq, dk), BF16) + _nbytes((seq, dv), BF16)
              + _nbytes((tq, dv), BF16))
    return pl.pallas_call(
        functools.partial(_mla_attn_kernel, tk=tk),
        grid=(batch, heads, nq),
        in_specs=[pl.BlockSpec((tq, dk), lambda b, h, i: (b * nq + i, h)),
                  pl.BlockSpec((seq, dk), lambda b, h, i: (b, h)),
                  pl.BlockSpec((seq, dv), lambda b, h, i: (b, h))],
        out_specs=pl.BlockSpec((tq, dv), lambda b, h, i: (b * nq + i, h)),
        out_shape=jax.ShapeDtypeStruct((batch * seq, heads * dv), BF16),
        compiler_params=_params(("parallel", "parallel", "arbitrary"),
                                blocks + 4 * _nbytes((tq, tk), F32)),
        name="mla_attn",
    )(q, k, v)


def _bias_master_kernel(vec_ref, o_ref, *, tq):
    width = vec_ref.shape[-1]
    m = jnp.broadcast_to(vec_ref[0], (tq, width))
    left = (tq - 1) - lax.broadcasted_iota(jnp.int32, (tq, 1), 0)
    step = 1
    while step < tq:
        shifted = pltpu.roll(m, width - step, axis=1)
        m = jnp.where((left & step) != 0, shifted, m)
        step *= 2
    o_ref[0] = m


def _bias_master(vec, tq):
    heads, _, width = vec.shape
    blocks = _nbytes((1, width), F32) + _nbytes((tq, width), F32)
    return pl.pallas_call(
        functools.partial(_bias_master_kernel, tq=tq),
        grid=(heads,),
        in_specs=[pl.BlockSpec((1, 1, width), lambda h: (h, 0, 0))],
        out_specs=pl.BlockSpec((1, tq, width), lambda h: (h, 0, 0)),
        out_shape=jax.ShapeDtypeStruct((heads, tq, width), F32),
        compiler_params=_params(("parallel",), blocks + 2 * _nbytes((tq, width), F32)),
        name="bias_master",
    )(vec)


def _diff_attn_kernel(lam_ref, g_ref, q_ref, k_ref, v_ref, bias_ref, o_ref,
                      *, tk, lambda_init):
    tq = q_ref.shape[0]
    seq = k_ref.shape[0]
    sub = bias_ref.shape[1]
    lam = lam_ref[...]
    lam_full = (jnp.exp(jnp.sum(lam[0:1] * lam[1:2], axis=-1, keepdims=True))
                - jnp.exp(jnp.sum(lam[2:3] * lam[3:4], axis=-1, keepdims=True)) + lambda_init)
    row0 = pl.program_id(2) * tq

    def bias_fn(c):
        parts = []
        for j in range(tq // sub):
            off = pl.multiple_of(seq - sub - row0 - j * sub + c * tk, LANES_V7X)
            parts.append(bias_ref[0, :, pl.ds(off, tk)])
        return parts[0] if len(parts) == 1 else jnp.concatenate(parts, axis=0)

    outs = []
    for g in range(2):
        cols = slice(g * HEAD_DIM, (g + 1) * HEAD_DIM)
        acc, l = _online_softmax_pv(q_ref[:, cols], k_ref, v_ref, cols, tk, bias_fn)
        outs.append(acc / l)
    o = outs[0] - lam_full * outs[1]
    o = _rms_norm(o, g_ref[...]) * (1.0 - lambda_init)
    o_ref[...] = o.astype(o_ref.dtype)


def _diff_attn(qkv, lam, sub_norm, bias_master, batch, seq, heads, layer_idx, tq=512, tk=512):
    sub = bias_master.shape[1]
    tq, tk = _tile(seq, tq), _tile(seq, tk)
    assert tq % sub == 0 and sub % LANES_V7X == 0, (tq, sub)
    nq = seq // tq
    d2 = 2 * HEAD_DIM
    lambda_init = 0.8 - 0.6 * math.exp(-0.3 * layer_idx)
    blocks = (2 * _nbytes((tq, d2), BF16) + 2 * _nbytes((seq, d2), BF16)
              + _nbytes((sub, 2 * seq), F32))
    return pl.pallas_call(
        functools.partial(_diff_attn_kernel, tk=tk, lambda_init=lambda_init),
        grid=(heads, batch, nq),
        in_specs=[pl.BlockSpec(lam.shape, lambda h, b, i: (0, 0)),
                  pl.BlockSpec((1, d2), lambda h, b, i: (0, 0)),
                  pl.BlockSpec((tq, d2), lambda h, b, i: (b * nq + i, h)),
                  pl.BlockSpec((seq, d2), lambda h, b, i: (b, heads + h)),
                  pl.BlockSpec((seq, d2), lambda h, b, i: (b, 2 * heads + h)),
                  pl.BlockSpec((1, sub, 2 * seq), lambda h, b, i: (h, 0, 0))],
        out_specs=pl.BlockSpec((tq, d2), lambda h, b, i: (b * nq + i, h)),
        out_shape=jax.ShapeDtypeStruct((batch * seq, heads * d2), BF16),
        compiler_params=_params(("parallel", "parallel", "arbitrary"),
                                blocks + 4 * _nbytes((tq, tk), F32)),
        name="diff_attn",
    )(lam, sub_norm, qkv, qkv, qkv, bias_master)


def _proj_ln_kernel(a_ref, w_ref, x_ref, g_ref, b_ref, y_ref, yb_ref, *, alpha, sub):
    g, b = g_ref[...], b_ref[...]
    for r in range(a_ref.shape[0] // sub):
        rows = slice(r * sub, (r + 1) * sub)
        y = alpha * x_ref[rows, :] + _dot(a_ref[rows, :], w_ref[...])
        y = _layer_norm(y, g, b)
        y_ref[rows, :] = y
        yb_ref[rows, :] = y.astype(yb_ref.dtype)


def _proj_ln(a, w, layer, x, g, b, alpha, tm=512, sub=256):
    m, k = a.shape
    d = w.shape[2]
    tm = _tile(m, tm)
    sub = math.gcd(tm, sub)
    blocks = (_nbytes((tm, k), BF16) + _nbytes((k, d), BF16) + 2 * _nbytes((tm, d), F32)
              + _nbytes((tm, d), BF16))
    row = lambda i: (i, 0)
    fixed = lambda i: (0, 0)
    return pl.pallas_call(
        functools.partial(_proj_ln_kernel, alpha=alpha, sub=sub),
        grid=(m // tm,),
        in_specs=[pl.BlockSpec((tm, k), row), _layer_spec((k, d), fixed, layer),
                  pl.BlockSpec((tm, d), row), pl.BlockSpec((1, d), fixed),
                  pl.BlockSpec((1, d), fixed)],
        out_specs=[pl.BlockSpec((tm, d), row), pl.BlockSpec((tm, d), row)],
        out_shape=[jax.ShapeDtypeStruct((m, d), F32), jax.ShapeDtypeStruct((m, d), BF16)],
        compiler_params=_params(("parallel",), blocks + 2 * _nbytes((sub, d), F32)),
        name="proj_ln",
    )(a, w, x, g, b)


def _ffn_up_kernel(x_ref, wg_ref, wu_ref, h_ref):
    x = x_ref[...]
    g = _dot(x, wg_ref[...])
    u = _dot(x, wu_ref[...])
    h_ref[...] = (g * jax.nn.sigmoid(g) * u).astype(h_ref.dtype)


def _ffn_up(xb, w_in, layer, tm=1024, tn=512):
    m, k = xb.shape
    ff = w_in.shape[2] // 2
    tm, tn = _tile(m, tm), _tile(ff, tn)
    n_tiles = ff // tn
    blocks = _nbytes((tm, k), BF16) + 2 * _nbytes((k, tn), BF16) + _nbytes((tm, tn), BF16)
    return pl.pallas_call(
        _ffn_up_kernel,
        grid=(m // tm, n_tiles),
        in_specs=[pl.BlockSpec((tm, k), lambda i, j: (i, 0)),
                  _layer_spec((k, tn), lambda i, j: (0, j), layer),
                  _layer_spec((k, tn), lambda i, j: (0, j + n_tiles), layer)],
        out_specs=pl.BlockSpec((tm, tn), lambda i, j: (i, j)),
        out_shape=jax.ShapeDtypeStruct((m, ff), BF16),
        compiler_params=_params(("parallel", "arbitrary"), blocks + 3 * _nbytes((tm, tn), F32)),
        name="ffn_up",
    )(xb, w_in, w_in)


def _ffn_down_ln_kernel(h_ref, w_ref, x_ref, g_ref, b_ref, y_ref, yb_ref, *, alpha, sub):
    kk = pl.program_id(1)
    last = pl.num_programs(1) - 1

    @pl.when(kk == 0)
    def _():
        y_ref[...] = _dot(h_ref[...], w_ref[...])

    @pl.when(jnp.logical_and(kk > 0, kk < last))
    def _():
        y_ref[...] += _dot(h_ref[...], w_ref[...])

    @pl.when(kk == last)
    def _():
        g, b = g_ref[...], b_ref[...]
        for r in range(h_ref.shape[0] // sub):
            rows = slice(r * sub, (r + 1) * sub)
            y = alpha * x_ref[rows, :] + y_ref[rows, :] + _dot(h_ref[rows, :], w_ref[...])
            y = _layer_norm(y, g, b)
            y_ref[rows, :] = y
            yb_ref[rows, :] = y.astype(yb_ref.dtype)


def _ffn_down_ln(h, w, layer, x, g, b, alpha, tm=1024, tk=512, sub=256):
    m, ff = h.shape
    d = w.shape[2]
    tm, tk = _tile(m, tm), _tile(ff, tk)
    sub = math.gcd(tm, sub)
    assert ff // tk >= 2, (ff, tk)
    blocks = (_nbytes((tm, tk), BF16) + _nbytes((tk, d), BF16) + 2 * _nbytes((tm, d), F32)
              + _nbytes((tm, d), BF16))
    row = lambda i, kk: (i, 0)
    fixed = lambda i, kk: (0, 0)
    return pl.pallas_call(
        functools.partial(_ffn_down_ln_kernel, alpha=alpha, sub=sub),
        grid=(m // tm, ff // tk),
        in_specs=[pl.BlockSpec((tm, tk), lambda i, kk: (i, kk)),
                  _layer_spec((tk, d), lambda i, kk: (kk, 0), layer),
                  pl.BlockSpec((tm, d), row), pl.BlockSpec((1, d), fixed),
                  pl.BlockSpec((1, d), fixed)],
        out_specs=[pl.BlockSpec((tm, d), row), pl.BlockSpec((tm, d), row)],
        out_shape=[jax.ShapeDtypeStruct((m, d), F32), jax.ShapeDtypeStruct((m, d), BF16)],
        compiler_params=_params(("parallel", "arbitrary"), blocks + 2 * _nbytes((sub, d), F32)),
        name="ffn_down_ln",
    )(h, w, x, g, b)


def _ple_kernel(xb_ref, wg_ref, p_ref, wp_ref, x_ref, y_ref, yb_ref):
    gate = jax.nn.sigmoid(_dot(xb_ref[...], wg_ref[...]))
    proj = _dot(p_ref[...].astype(BF16), wp_ref[...])
    y = x_ref[...] + gate * proj
    y_ref[...] = y
    yb_ref[...] = y.astype(yb_ref.dtype)


def _ple(xb, x, p, w_gate, w_proj, layer, tm=512):
    m, d = x.shape
    pd = p.shape[2]
    tm = _tile(m, tm)
    blocks = (_nbytes((tm, d), BF16) + _nbytes((d, d), BF16) + _nbytes((tm, pd), F32)
              + _nbytes((pd, d), BF16) + 2 * _nbytes((tm, d), F32) + _nbytes((tm, d), BF16))
    row = lambda i: (i, 0)
    fixed = lambda i: (0, 0)
    return pl.pallas_call(
        _ple_kernel,
        grid=(m // tm,),
        in_specs=[pl.BlockSpec((tm, d), row),
                  _layer_spec((d, d), fixed, layer),
                  _layer_spec((tm, pd), row, layer),
                  _layer_spec((pd, d), fixed, layer),
                  pl.BlockSpec((tm, d), row)],
        out_specs=[pl.BlockSpec((tm, d), row), pl.BlockSpec((tm, d), row)],
        out_shape=[jax.ShapeDtypeStruct((m, d), F32), jax.ShapeDtypeStruct((m, d), BF16)],
        compiler_params=_params(("parallel",), blocks + 2 * _nbytes((tm, d), F32)),
        name="ple",
    )(xb, w_gate, p, w_proj, x)


def _rope_tables(seq):
    half = ROPE_DIM // 2
    pos = jnp.arange(seq, dtype=F32)
    inv = 1.0 / (ROPE_THETA ** (jnp.arange(0, ROPE_DIM, 2, dtype=F32) / ROPE_DIM))
    ang = pos[:, None] * inv[None, :]
    cos, sin = jnp.cos(ang), jnp.sin(ang)
    zeros = jnp.zeros((seq, half), F32)
    pad = jnp.zeros((seq, LANES_V7X - ROPE_DIM), F32)
    cos_t = jnp.concatenate([cos, cos, pad], axis=1)
    sin_a = jnp.concatenate([-sin, zeros, pad], axis=1)
    sin_b = jnp.concatenate([zeros, sin, pad], axis=1)
    return cos_t, sin_a, sin_b


def _t5_bucket(rel):
    nb = REL_BUCKETS // 2
    max_exact = nb // 2
    ret = (rel > 0).astype(jnp.int32) * nb
    n = jnp.abs(rel)
    nf = jnp.maximum(n, 1).astype(F32)
    large = max_exact + (jnp.log(nf / max_exact) / math.log(REL_MAX_DIST / max_exact)
                         * (nb - max_exact)).astype(jnp.int32)
    large = jnp.minimum(large, nb - 1)
    return ret + jnp.where(n < max_exact, n, large)


def _bias_vector(rel_bias, seq):
    rel = jnp.arange(-(seq - 1), seq + 1, dtype=jnp.int32)
    vec = rel_bias.astype(F32)[_t5_bucket(rel)] * LOG2E
    return jnp.transpose(vec)[:, None, :]


def _mla_weights(w_in, w_uq, w_ukv, heads, q_rank, kv_rank):
    d = w_in.shape[0]
    w_cat = jnp.concatenate(
        [w_in, jnp.zeros((d, LANES_V7X - ROPE_DIM), w_in.dtype)], axis=1).astype(BF16)
    wq = w_uq.reshape(q_rank, heads, HEAD_DIM + ROPE_DIM)
    wq = jnp.pad(wq, ((0, 0), (0, 0), (0, 2 * HEAD_DIM - HEAD_DIM - ROPE_DIM)))
    wq = wq.reshape(q_rank, heads * 2 * HEAD_DIM).astype(BF16)
    wkv = w_ukv.reshape(kv_rank, heads, 2 * HEAD_DIM)
    wk = wkv[:, :, :HEAD_DIM].reshape(kv_rank, heads * HEAD_DIM).astype(BF16)
    wv = wkv[:, :, HEAD_DIM:].reshape(kv_rank, heads * HEAD_DIM).astype(BF16)
    return w_cat, wq, wk, wv


def kernel(x, p, mla_w_in, mla_q_norm, mla_kv_norm, mla_w_uq, mla_w_ukv, mla_w_o, diff_w_in, diff_lambda, diff_sub_norm, diff_w_o, rel_bias, ln_g, ln_b, ffn_w_in, ffn_w_out, ple_w_gate, ple_w_proj):
    batch, seq, d = x.shape
    depth = p.shape[0]
    tokens = batch * seq
    alpha = (2 * depth) ** 0.25
    mla_heads = mla_w_o.shape[1] // HEAD_DIM
    diff_heads = diff_w_o.shape[1] // (2 * HEAD_DIM)
    q_rank, kv_rank = mla_q_norm.shape[1], mla_kv_norm.shape[1]
    mla_scale = (HEAD_DIM + ROPE_DIM) ** -0.5 * LOG2E
    diff_scale = HEAD_DIM ** -0.5 * LOG2E
    diff_q_cols = diff_heads * 2 * HEAD_DIM
    diff_col_scale = jnp.concatenate(
        [jnp.full((1, diff_q_cols), diff_scale, F32),
         jnp.ones((1, diff_w_in.shape[2] - diff_q_cols), F32)], axis=1)

    rope_tabs = _rope_tables(seq)
    rope_tabs_q = tuple(t * mla_scale for t in rope_tabs)
    bias_master = _bias_master(_bias_vector(rel_bias, seq), _tile(seq, 256))

    mla_w_o_b, diff_w_o_b = mla_w_o.astype(BF16), diff_w_o.astype(BF16)
    diff_w_in_b = diff_w_in.astype(BF16)
    ffn_w_in_b, ffn_w_out_b = ffn_w_in.astype(BF16), ffn_w_out.astype(BF16)
    ple_w_gate_b, ple_w_proj_b = ple_w_gate.astype(BF16), ple_w_proj.astype(BF16)
    p_rows = p.reshape(depth, tokens, p.shape[-1])

    xf = x.reshape(tokens, d)
    xb = xf.astype(BF16)
    for i in range(depth):
        j = i // 2
        if i % 2 == 0:
            w_cat, wq, wk, wv = _mla_weights(mla_w_in[j], mla_w_uq[j], mla_w_ukv[j],
                                             mla_heads, q_rank, kv_rank)
            cq, ckv, kr = _mla_in(xb, w_cat, mla_q_norm[j][None], mla_kv_norm[j][None],
                                  rope_tabs, seq)
            q = _mla_q(cq, wq, rope_tabs_q, seq, mla_scale)
            k, v = _mla_kv(ckv, wk, wv, kr)
            o = _mla_attn(q, k, v, batch, seq, mla_heads)
            w_o = mla_w_o_b
        else:
            qkv = _matmul(xb, diff_w_in_b, j, diff_col_scale, BF16)
            o = _diff_attn(qkv, diff_lambda[j], diff_sub_norm[j][None], bias_master,
                           batch, seq, diff_heads, i)
            w_o = diff_w_o_b
        xf, xb = _proj_ln(o, w_o, j, xf, ln_g[i, 0][None], ln_b[i, 0][None], alpha)
        h = _ffn_up(xb, ffn_w_in_b, i)
        xf, xb = _ffn_down_ln(h, ffn_w_out_b, i, xf, ln_g[i, 1][None], ln_b[i, 1][None], alpha)
        xf, xb = _ple(xb, xf, p_rows, ple_w_gate_b, ple_w_proj_b, i)
    return xf.reshape(batch, seq, d)
```

```python
import functools
import math

import jax
import jax.numpy as jnp
from jax import lax
from jax.experimental import pallas as pl
from jax.experimental.pallas import tpu as pltpu

LANES_V7X = 128
VMEM_BYTES_V7X = 64 * 1024 * 1024
VMEM_LIMIT_CAP = VMEM_BYTES_V7X - 8 * 1024 * 1024

HEAD_DIM = 128
ROPE_DIM = 64
ROPE_THETA = 10000.0
REL_BUCKETS = 32
REL_MAX_DIST = 128
LN_EPS = 1e-5
RMS_EPS = 1e-6
LOG2E = math.log2(math.e)

BF16 = jnp.bfloat16
F32 = jnp.float32


def _vmem_limit(block_bytes):
    return int(min(VMEM_LIMIT_CAP, max(32 * 1024 * 1024, 2 * block_bytes + 16 * 1024 * 1024)))


def _nbytes(shape, dtype):
    return math.prod(shape) * jnp.dtype(dtype).itemsize


def _params(semantics, block_bytes):
    return pltpu.CompilerParams(dimension_semantics=semantics,
                                vmem_limit_bytes=_vmem_limit(block_bytes))


def _tile(n, want):
    if n <= want:
        return n
    t = want - want % LANES_V7X
    while t > LANES_V7X and n % t:
        t -= LANES_V7X
    assert n % t == 0, (n, want)
    return t


def _layer_spec(block, index_map, layer):
    return pl.BlockSpec((None,) + tuple(block), lambda *g: (layer,) + tuple(index_map(*g)))


def _dot(a, b):
    return jnp.dot(a, b, preferred_element_type=F32)


def _dot_nt(a, b):
    return lax.dot_general(a, b, (((1,), (1,)), ((), ())), preferred_element_type=F32)


def _layer_norm(y, g, b):
    mu = jnp.mean(y, axis=-1, keepdims=True)
    yc = y - mu
    var = jnp.mean(yc * yc, axis=-1, keepdims=True)
    return yc * lax.rsqrt(var + LN_EPS) * g + b


def _rms_norm(y, g):
    return y * lax.rsqrt(jnp.mean(y * y, axis=-1, keepdims=True) + RMS_EPS) * g


def _rope_half(r, cos_t, sin_a, sin_b):
    up = pltpu.roll(r, LANES_V7X - ROPE_DIM // 2, axis=1)
    down = pltpu.roll(r, ROPE_DIM // 2, axis=1)
    return r * cos_t + up * sin_a + down * sin_b


def _matmul_kernel(x_ref, w_ref, cs_ref, o_ref):
    o_ref[...] = (_dot(x_ref[...], w_ref[...]) * cs_ref[...]).astype(o_ref.dtype)


def _matmul(x, w, layer, col_scale, out_dtype, tm=1024, tn=1024):
    m, k = x.shape
    n = w.shape[2]
    tm, tn = _tile(m, tm), _tile(n, tn)
    blocks = _nbytes((tm, k), x.dtype) + _nbytes((k, tn), w.dtype) + _nbytes((tm, tn), out_dtype)
    return pl.pallas_call(
        _matmul_kernel,
        grid=(m // tm, n // tn),
        in_specs=[pl.BlockSpec((tm, k), lambda i, j: (i, 0)),
                  _layer_spec((k, tn), lambda i, j: (0, j), layer),
                  pl.BlockSpec((1, tn), lambda i, j: (0, j))],
        out_specs=pl.BlockSpec((tm, tn), lambda i, j: (i, j)),
        out_shape=jax.ShapeDtypeStruct((m, n), out_dtype),
        compiler_params=_params(("parallel", "arbitrary"), blocks + _nbytes((tm, tn), F32)),
        name="matmul",
    )(x, w, col_scale)


def _mla_in_kernel(x_ref, w_ref, gq_ref, gkv_ref, cos_ref, sa_ref, sb_ref,
                   cq_ref, ckv_ref, kr_ref, *, q_rank, kv_rank, sub):
    gq, gkv = gq_ref[...], gkv_ref[...]
    for r in range(x_ref.shape[0] // sub):
        rows = slice(r * sub, (r + 1) * sub)
        h = _dot(x_ref[rows, :], w_ref[...])
        cq_ref[rows, :] = _rms_norm(h[:, :q_rank], gq).astype(cq_ref.dtype)
        ckv_ref[rows, :] = _rms_norm(h[:, q_rank:q_rank + kv_rank], gkv).astype(ckv_ref.dtype)
        kr = _rope_half(h[:, q_rank + kv_rank:], cos_ref[rows, :], sa_ref[rows, :], sb_ref[rows, :])
        kr_ref[rows, :] = kr.astype(kr_ref.dtype)


def _mla_in(xb, w_cat, gq, gkv, rope_tabs, seq, tm=1024, sub=256):
    m, k = xb.shape
    q_rank, kv_rank = gq.shape[1], gkv.shape[1]
    n = w_cat.shape[1]
    tm = _tile(seq, tm)
    sub = math.gcd(tm, sub)
    per_seq = seq // tm
    blocks = (_nbytes((tm, k), BF16) + _nbytes((k, n), BF16) + 3 * _nbytes((tm, LANES_V7X), F32)
              + _nbytes((tm, n), BF16))
    tab_spec = pl.BlockSpec((tm, LANES_V7X), lambda i: (i % per_seq, 0))
    return pl.pallas_call(
        functools.partial(_mla_in_kernel, q_rank=q_rank, kv_rank=kv_rank, sub=sub),
        grid=(m // tm,),
        in_specs=[pl.BlockSpec((tm, k), lambda i: (i, 0)),
                  pl.BlockSpec((k, n), lambda i: (0, 0)),
                  pl.BlockSpec((1, q_rank), lambda i: (0, 0)),
                  pl.BlockSpec((1, kv_rank), lambda i: (0, 0)),
                  tab_spec, tab_spec, tab_spec],
        out_specs=[pl.BlockSpec((tm, q_rank), lambda i: (i, 0)),
                   pl.BlockSpec((tm, kv_rank), lambda i: (i, 0)),
                   pl.BlockSpec((tm, LANES_V7X), lambda i: (i, 0))],
        out_shape=[jax.ShapeDtypeStruct((m, q_rank), BF16),
                   jax.ShapeDtypeStruct((m, kv_rank), BF16),
                   jax.ShapeDtypeStruct((m, LANES_V7X), BF16)],
        compiler_params=_params(("parallel",), blocks + _nbytes((tm, n), F32)),
        name="mla_in",
    )(xb, w_cat, gq, gkv, *rope_tabs)


def _mla_q_kernel(c_ref, w_ref, cos_ref, sa_ref, sb_ref, q_ref, *, heads_per_step, scale):
    acc = _dot(c_ref[...], w_ref[...])
    cos_t, sa, sb = cos_ref[...], sa_ref[...], sb_ref[...]
    for h in range(heads_per_step):
        base = 2 * HEAD_DIM * h
        q_ref[:, base:base + HEAD_DIM] = (acc[:, base:base + HEAD_DIM] * scale).astype(q_ref.dtype)
        rope = _rope_half(acc[:, base + HEAD_DIM:base + 2 * HEAD_DIM], cos_t, sa, sb)
        q_ref[:, base + HEAD_DIM:base + 2 * HEAD_DIM] = rope.astype(q_ref.dtype)


def _mla_q(cq, w_uq, rope_tabs, seq, scale, tm=1024, heads_per_step=8):
    m, k = cq.shape
    n = w_uq.shape[1]
    tm = _tile(seq, tm)
    per_seq = seq // tm
    tn = min(n, heads_per_step * 2 * HEAD_DIM)
    heads_per_step = tn // (2 * HEAD_DIM)
    blocks = (_nbytes((tm, k), BF16) + _nbytes((k, tn), BF16) + 3 * _nbytes((tm, LANES_V7X), F32)
              + _nbytes((tm, tn), BF16))
    tab_spec = pl.BlockSpec((tm, LANES_V7X), lambda i, j: (i % per_seq, 0))
    return pl.pallas_call(
        functools.partial(_mla_q_kernel, heads_per_step=heads_per_step, scale=scale),
        grid=(m // tm, n // tn),
        in_specs=[pl.BlockSpec((tm, k), lambda i, j: (i, 0)),
                  pl.BlockSpec((k, tn), lambda i, j: (0, j)),
                  tab_spec, tab_spec, tab_spec],
        out_specs=pl.BlockSpec((tm, tn), lambda i, j: (i, j)),
        out_shape=jax.ShapeDtypeStruct((m, n), BF16),
        compiler_params=_params(("parallel", "arbitrary"), blocks + _nbytes((tm, tn), F32)),
        name="mla_q",
    )(cq, w_uq, *rope_tabs)


def _mla_kv_kernel(c_ref, wk_ref, wv_ref, kr_ref, k_ref, v_ref, *, heads_per_step):
    c = c_ref[...]
    kn = _dot(c, wk_ref[...])
    v_ref[...] = _dot(c, wv_ref[...]).astype(v_ref.dtype)
    kr = kr_ref[...]
    for h in range(heads_per_step):
        base = 2 * HEAD_DIM * h
        k_ref[:, base:base + HEAD_DIM] = kn[:, HEAD_DIM * h:HEAD_DIM * (h + 1)].astype(k_ref.dtype)
        k_ref[:, base + HEAD_DIM:base + 2 * HEAD_DIM] = kr


def _mla_kv(ckv, w_k, w_v, kr, tm=1024, heads_per_step=8):
    m, k = ckv.shape
    n = w_k.shape[1]
    tm = _tile(m, tm)
    tn = min(n, heads_per_step * HEAD_DIM)
    heads_per_step = tn // HEAD_DIM
    blocks = (_nbytes((tm, k), BF16) + 2 * _nbytes((k, tn), BF16) + _nbytes((tm, LANES_V7X), BF16)
              + 3 * _nbytes((tm, tn), BF16))
    return pl.pallas_call(
        functools.partial(_mla_kv_kernel, heads_per_step=heads_per_step),
        grid=(m // tm, n // tn),
        in_specs=[pl.BlockSpec((tm, k), lambda i, j: (i, 0)),
                  pl.BlockSpec((k, tn), lambda i, j: (0, j)),
                  pl.BlockSpec((k, tn), lambda i, j: (0, j)),
                  pl.BlockSpec((tm, LANES_V7X), lambda i, j: (i, 0))],
        out_specs=[pl.BlockSpec((tm, 2 * tn), lambda i, j: (i, j)),
                   pl.BlockSpec((tm, tn), lambda i, j: (i, j))],
        out_shape=[jax.ShapeDtypeStruct((m, 2 * n), BF16),
                   jax.ShapeDtypeStruct((m, n), BF16)],
        compiler_params=_params(("parallel", "arbitrary"), blocks + 2 * _nbytes((tm, tn), F32)),
        name="mla_kv",
    )(ckv, w_k, w_v, kr)


def _online_softmax_pv(q, k_ref, v_ref, k_cols, tk, bias_fn=None):
    tq = q.shape[0]
    seq = k_ref.shape[0]
    groups = tk // LANES_V7X
    m = jnp.full((tq, 1), -jnp.inf, F32)
    l_acc = jnp.zeros((tq, LANES_V7X), F32)
    acc = jnp.zeros((tq, v_ref.shape[1]), F32)
    for c in range(seq // tk):
        s = _dot_nt(q, k_ref[c * tk:(c + 1) * tk, k_cols])
        if bias_fn is not None:
            s = s + bias_fn(c)
        m_lane = s[:, :LANES_V7X]
        for g in range(1, groups):
            m_lane = jnp.maximum(m_lane, s[:, g * LANES_V7X:(g + 1) * LANES_V7X])
        m_new = jnp.maximum(m, jnp.max(m_lane, axis=-1, keepdims=True))
        alpha = jnp.exp2(m - m_new)
        p = jnp.exp2(s - m_new)
        p_sum = p[:, :LANES_V7X]
        for g in range(1, groups):
            p_sum = p_sum + p[:, g * LANES_V7X:(g + 1) * LANES_V7X]
        l_acc = alpha * l_acc + p_sum
        acc = alpha * acc + _dot(p.astype(BF16), v_ref[c * tk:(c + 1) * tk, :])
        m = m_new
    return acc, jnp.sum(l_acc, axis=-1, keepdims=True)


def _mla_attn_kernel(q_ref, k_ref, v_ref, o_ref, *, tk):
    acc, l = _online_softmax_pv(q_ref[...], k_ref, v_ref, slice(None), tk)
    o_ref[...] = (acc / l).astype(o_ref.dtype)


def _mla_attn(q, k, v, batch, seq, heads, tq=2048, tk=256):
    tq, tk = _tile(seq, tq), _tile(seq, tk)
    nq = seq // tq
    dk, dv = 2 * HEAD_DIM, HEAD_DIM
    blocks = (_nbytes((tq, dk), BF16) + _nbytes((seq, dk), BF16) + _nbytes((seq, dv), BF16)
              + _nbytes((tq, dv), BF16))
    return pl.pallas_call(
        functools.partial(_mla_attn_kernel, tk=tk),
        grid=(batch, heads, nq),
        in_specs=[pl.BlockSpec((tq, dk), lambda b, h, i: (b * nq + i, h)),
                  pl.BlockSpec((seq, dk), lambda b, h, i: (b, h)),
                  pl.BlockSpec((seq, dv), lambda b, h, i: (b, h))],
        out_specs=pl.BlockSpec((tq, dv), lambda b, h, i: (b * nq + i, h)),
        out_shape=jax.ShapeDtypeStruct((batch * seq, heads * dv), BF16),
        compiler_params=_params(("parallel", "parallel", "arbitrary"),
                                blocks + 4 * _nbytes((tq, tk), F32)),
        name="mla_attn",
    )(q, k, v)


def _bias_master_kernel(vec_ref, o_ref, *, tq):
    width = vec_ref.shape[-1]
    m = jnp.broadcast_to(vec_ref[0], (tq, width))
    left = (tq - 1) - lax.broadcasted_iota(jnp.int32, (tq, 1), 0)
    step = 1
    while step < tq:
        shifted = pltpu.roll(m, width - step, axis=1)
        m = jnp.where((left & step) != 0, shifted, m)
        step *= 2
    o_ref[0] = m


def _bias_master(vec, tq):
    heads, _, width = vec.shape
    blocks = _nbytes((1, width), F32) + _nbytes((tq, width), F32)
    return pl.pallas_call(
        functools.partial(_bias_master_kernel, tq=tq),
        grid=(heads,),
        in_specs=[pl.BlockSpec((1, 1, width), lambda h: (h, 0, 0))],
        out_specs=pl.BlockSpec((1, tq, width), lambda h: (h, 0, 0)),
        out_shape=jax.ShapeDtypeStruct((heads, tq, width), F32),
        compiler_params=_params(("parallel",), blocks + 2 * _nbytes((tq, width), F32)),
        name="bias_master",
    )(vec)


def _diff_attn_kernel(lam_ref, g_ref, q_ref, k_ref, v_ref, bias_ref, o_ref, z_ref, m_ref,
                      *, tk, nq, n_tiles, lambda_init):
    t = pl.program_id(0)
    tq = q_ref.shape[0]
    seq = k_ref.shape[0]
    sub = bias_ref.shape[1]
    groups = tk // LANES_V7X
    lam = lam_ref[...]
    lam_full = (jnp.exp(jnp.sum(lam[0:1] * lam[1:2], axis=-1, keepdims=True))
                - jnp.exp(jnp.sum(lam[2:3] * lam[3:4], axis=-1, keepdims=True)) + lambda_init)
    row0 = (jnp.minimum(t, n_tiles - 1) % nq) * tq

    @pl.when(t == 0)
    def _():
        z_ref[...] = jnp.zeros(z_ref.shape, F32)
        m_ref[...] = jnp.zeros(m_ref.shape, F32)

    def bias_fn(c):
        parts = []
        for j in range(tq // sub):
            off = pl.multiple_of(seq - sub - row0 - j * sub + c * tk, LANES_V7X)
            parts.append(bias_ref[0, :, pl.ds(off, tk)])
        return parts[0] if len(parts) == 1 else jnp.concatenate(parts, axis=0)

    qs = [q_ref[:, g * HEAD_DIM:(g + 1) * HEAD_DIM] for g in range(2)]
    ms = [m_ref[g] for g in range(2)]
    m_lane = [jnp.full((tq, LANES_V7X), -jnp.inf, F32) for _ in range(2)]
    l_acc = [jnp.zeros((tq, LANES_V7X), F32) for _ in range(2)]
    acc = [jnp.zeros((tq, v_ref.shape[1]), F32) for _ in range(2)]
    for c in range(seq // tk):
        bias = bias_fn(c)
        for g in range(2):
            ps = []
            for j in range(groups):
                lo = c * tk + j * LANES_V7X
                pg = jnp.exp2(z_ref[g, :, lo:lo + LANES_V7X] - ms[g])
                l_acc[g] = l_acc[g] + pg
                ps.append(pg.astype(BF16))
            acc[g] = acc[g] + _dot(jnp.concatenate(ps, axis=1), v_ref[c * tk:(c + 1) * tk, :])
            z = _dot_nt(qs[g], k_ref[c * tk:(c + 1) * tk, g * HEAD_DIM:(g + 1) * HEAD_DIM]) + bias
            z_ref[g, :, c * tk:(c + 1) * tk] = z
            for j in range(groups):
                m_lane[g] = jnp.maximum(m_lane[g], z[:, j * LANES_V7X:(j + 1) * LANES_V7X])
    outs = []
    for g in range(2):
        m_ref[g] = jnp.broadcast_to(jnp.max(m_lane[g], axis=-1, keepdims=True), (tq, LANES_V7X))
        outs.append(acc[g] / jnp.sum(l_acc[g], axis=-1, keepdims=True))
    o = outs[0] - lam_full * outs[1]
    o = _rms_norm(o, g_ref[...]) * (1.0 - lambda_init)
    o_ref[...] = o.astype(o_ref.dtype)


def _diff_attn(qkv, lam, sub_norm, bias_master, batch, seq, heads, layer_idx, tq=512, tk=256):
    sub = bias_master.shape[1]
    tq, tk = _tile(seq, tq), _tile(seq, tk)
    assert tq % sub == 0 and sub % LANES_V7X == 0, (tq, sub)
    nq = seq // tq
    d2 = 2 * HEAD_DIM
    n_tiles = heads * batch * nq
    lambda_init = 0.8 - 0.6 * math.exp(-0.3 * layer_idx)

    def tile(u):
        return u // (batch * nq), (u // nq) % batch, u % nq

    def prepared(t):
        return tile(jnp.minimum(t, n_tiles - 1))

    def finished(t):
        return tile(jnp.maximum(t - 1, 0))

    def q_map(t):
        h, b, i = prepared(t)
        return (b * nq + i, h)

    def k_map(t):
        h, b, i = prepared(t)
        return (b, heads + h)

    def bias_map(t):
        h, b, i = prepared(t)
        return (h, 0, 0)

    def v_map(t):
        h, b, i = finished(t)
        return (b, 2 * heads + h)

    def o_map(t):
        h, b, i = finished(t)
        return (b * nq + i, h)

    blocks = (2 * _nbytes((tq, d2), BF16) + 2 * _nbytes((seq, d2), BF16)
              + _nbytes((sub, 2 * seq), F32))
    scratch = _nbytes((2, tq, seq), F32) + _nbytes((2, tq, LANES_V7X), F32)
    return pl.pallas_call(
        functools.partial(_diff_attn_kernel, tk=tk, nq=nq, n_tiles=n_tiles, lambda_init=lambda_init),
        grid=(n_tiles + 1,),
        in_specs=[pl.BlockSpec(lam.shape, lambda t: (0, 0)),
                  pl.BlockSpec((1, d2), lambda t: (0, 0)),
                  pl.BlockSpec((tq, d2), q_map),
                  pl.BlockSpec((seq, d2), k_map),
                  pl.BlockSpec((seq, d2), v_map),
                  pl.BlockSpec((1, sub, 2 * seq), bias_map)],
        out_specs=pl.BlockSpec((tq, d2), o_map),
        out_shape=jax.ShapeDtypeStruct((batch * seq, heads * d2), BF16),
        scratch_shapes=[pltpu.VMEM((2, tq, seq), F32), pltpu.VMEM((2, tq, LANES_V7X), F32)],
        compiler_params=_params(("arbitrary",), blocks + scratch // 2 + 2 * _nbytes((tq, tk), F32)),
        name="diff_attn",
    )(lam, sub_norm, qkv, qkv, qkv, bias_master)


def _proj_ln_kernel(a_ref, w_ref, x_ref, g_ref, b_ref, y_ref, yb_ref, *, alpha, sub):
    g, b = g_ref[...], b_ref[...]
    for r in range(a_ref.shape[0] // sub):
        rows = slice(r * sub, (r + 1) * sub)
        y = alpha * x_ref[rows, :] + _dot(a_ref[rows, :], w_ref[...])
        y = _layer_norm(y, g, b)
        y_ref[rows, :] = y
        yb_ref[rows, :] = y.astype(yb_ref.dtype)


def _proj_ln(a, w, layer, x, g, b, alpha, tm=512, sub=256):
    m, k = a.shape
    d = w.shape[2]
    tm = _tile(m, tm)
    sub = math.gcd(tm, sub)
    blocks = (_nbytes((tm, k), BF16) + _nbytes((k, d), BF16) + 2 * _nbytes((tm, d), F32)
              + _nbytes((tm, d), BF16))
    row = lambda i: (i, 0)
    fixed = lambda i: (0, 0)
    return pl.pallas_call(
        functools.partial(_proj_ln_kernel, alpha=alpha, sub=sub),
        grid=(m // tm,),
        in_specs=[pl.BlockSpec((tm, k), row), _layer_spec((k, d), fixed, layer),
                  pl.BlockSpec((tm, d), row), pl.BlockSpec((1, d), fixed),
                  pl.BlockSpec((1, d), fixed)],
        out_specs=[pl.BlockSpec((tm, d), row), pl.BlockSpec((tm, d), row)],
        out_shape=[jax.ShapeDtypeStruct((m, d), F32), jax.ShapeDtypeStruct((m, d), BF16)],
        compiler_params=_params(("parallel",), blocks + 2 * _nbytes((sub, d), F32)),
        name="proj_ln",
    )(a, w, x, g, b)


def _ffn_up_kernel(x_ref, wg_ref, wu_ref, h_ref):
    x = x_ref[...]
    g = _dot(x, wg_ref[...])
    u = _dot(x, wu_ref[...])
    h_ref[...] = (g * jax.nn.sigmoid(g) * u).astype(h_ref.dtype)


def _ffn_up(xb, w_in, layer, tm=1024, tn=512):
    m, k = xb.shape
    ff = w_in.shape[2] // 2
    tm, tn = _tile(m, tm), _tile(ff, tn)
    n_tiles = ff // tn
    blocks = _nbytes((tm, k), BF16) + 2 * _nbytes((k, tn), BF16) + _nbytes((tm, tn), BF16)
    return pl.pallas_call(
        _ffn_up_kernel,
        grid=(m // tm, n_tiles),
        in_specs=[pl.BlockSpec((tm, k), lambda i, j: (i, 0)),
                  _layer_spec((k, tn), lambda i, j: (0, j), layer),
                  _layer_spec((k, tn), lambda i, j: (0, j + n_tiles), layer)],
        out_specs=pl.BlockSpec((tm, tn), lambda i, j: (i, j)),
        out_shape=jax.ShapeDtypeStruct((m, ff), BF16),
        compiler_params=_params(("parallel", "arbitrary"), blocks + 3 * _nbytes((tm, tn), F32)),
        name="ffn_up",
    )(xb, w_in, w_in)


def _ffn_down_ln_kernel(h_ref, w_ref, x_ref, g_ref, b_ref, y_ref, yb_ref, *, alpha, sub):
    j = pl.program_id(1)
    last = pl.num_programs(1) - 1
    tm, tn = x_ref.shape
    d = y_ref.shape[1]

    @pl.when(j < last)
    def _():
        col = pl.multiple_of(j * tn, tn)
        y_ref[:, pl.ds(col, tn)] = alpha * x_ref[...] + _dot(h_ref[...], w_ref[...])

    @pl.when(j == last)
    def _():
        g, b = g_ref[...], b_ref[...]
        for r in range(tm // sub):
            rows = slice(r * sub, (r + 1) * sub)
            tail = alpha * x_ref[rows, :] + _dot(h_ref[rows, :], w_ref[...])
            y = tail if d == tn else jnp.concatenate([y_ref[rows, :d - tn], tail], axis=1)
            y = _layer_norm(y, g, b)
            y_ref[rows, :] = y
            yb_ref[rows, :] = y.astype(yb_ref.dtype)


def _ffn_down_ln(h, w, layer, x, g, b, alpha, tm=512, tn=512, sub=256):
    m, ff = h.shape
    d = w.shape[2]
    tm, tn = _tile(m, tm), _tile(d, tn)
    sub = math.gcd(tm, sub)
    blocks = (_nbytes((tm, ff), BF16) + _nbytes((ff, tn), BF16) + _nbytes((tm, tn), F32)
              + _nbytes((tm, d), F32) + _nbytes((tm, d), BF16))
    row = lambda i, j: (i, 0)
    fixed = lambda i, j: (0, 0)
    return pl.pallas_call(
        functools.partial(_ffn_down_ln_kernel, alpha=alpha, sub=sub),
        grid=(m // tm, d // tn),
        in_specs=[pl.BlockSpec((tm, ff), row),
                  _layer_spec((ff, tn), lambda i, j: (0, j), layer),
                  pl.BlockSpec((tm, tn), lambda i, j: (i, j)), pl.BlockSpec((1, d), fixed),
                  pl.BlockSpec((1, d), fixed)],
        out_specs=[pl.BlockSpec((tm, d), row), pl.BlockSpec((tm, d), row)],
        out_shape=[jax.ShapeDtypeStruct((m, d), F32), jax.ShapeDtypeStruct((m, d), BF16)],
        compiler_params=_params(("parallel", "arbitrary"), blocks + 2 * _nbytes((sub, d), F32)),
        name="ffn_down_ln",
    )(h, w, x, g, b)


def _ple_kernel(xb_ref, wg_ref, p_ref, wp_ref, x_ref, y_ref, yb_ref):
    gate = jax.nn.sigmoid(_dot(xb_ref[...], wg_ref[...]))
    proj = _dot(p_ref[...].astype(BF16), wp_ref[...])
    y = x_ref[...] + gate * proj
    y_ref[...] = y
    yb_ref[...] = y.astype(yb_ref.dtype)


def _ple(xb, x, p, w_gate, w_proj, layer, tm=512):
    m, d = x.shape
    pd = p.shape[2]
    tm = _tile(m, tm)
    blocks = (_nbytes((tm, d), BF16) + _nbytes((d, d), BF16) + _nbytes((tm, pd), F32)
              + _nbytes((pd, d), BF16) + 2 * _nbytes((tm, d), F32) + _nbytes((tm, d), BF16))
    row = lambda i: (i, 0)
    fixed = lambda i: (0, 0)
    return pl.pallas_call(
        _ple_kernel,
        grid=(m // tm,),
        in_specs=[pl.BlockSpec((tm, d), row),
                  _layer_spec((d, d), fixed, layer),
                  _layer_spec((tm, pd), row, layer),
                  _layer_spec((pd, d), fixed, layer),
                  pl.BlockSpec((tm, d), row)],
        out_specs=[pl.BlockSpec((tm, d), row), pl.BlockSpec((tm, d), row)],
        out_shape=[jax.ShapeDtypeStruct((m, d), F32), jax.ShapeDtypeStruct((m, d), BF16)],
        compiler_params=_params(("parallel",), blocks + 2 * _nbytes((tm, d), F32)),
        name="ple",
    )(xb, w_gate, p, w_proj, x)


def _rope_tables(seq):
    half = ROPE_DIM // 2
    pos = jnp.arange(seq, dtype=F32)
    inv = 1.0 / (ROPE_THETA ** (jnp.arange(0, ROPE_DIM, 2, dtype=F32) / ROPE_DIM))
    ang = pos[:, None] * inv[None, :]
    cos, sin = jnp.cos(ang), jnp.sin(ang)
    zeros = jnp.zeros((seq, half), F32)
    pad = jnp.zeros((seq, LANES_V7X - ROPE_DIM), F32)
    cos_t = jnp.concatenate([cos, cos, pad], axis=1)
    sin_a = jnp.concatenate([-sin, zeros, pad], axis=1)
    sin_b = jnp.concatenate([zeros, sin, pad], axis=1)
    return cos_t, sin_a, sin_b


def _t5_bucket(rel):
    nb = REL_BUCKETS // 2
    max_exact = nb // 2
    ret = (rel > 0).astype(jnp.int32) * nb
    n = jnp.abs(rel)
    nf = jnp.maximum(n, 1).astype(F32)
    large = max_exact + (jnp.log(nf / max_exact) / math.log(REL_MAX_DIST / max_exact)
                         * (nb - max_exact)).astype(jnp.int32)
    large = jnp.minimum(large, nb - 1)
    return ret + jnp.where(n < max_exact, n, large)


def _bias_vector(rel_bias, seq):
    rel = jnp.arange(-(seq - 1), seq + 1, dtype=jnp.int32)
    vec = rel_bias.astype(F32)[_t5_bucket(rel)] * LOG2E
    return jnp.transpose(vec)[:, None, :]


def _mla_weights(w_in, w_uq, w_ukv, heads, q_rank, kv_rank):
    d = w_in.shape[0]
    w_cat = jnp.concatenate(
        [w_in, jnp.zeros((d, LANES_V7X - ROPE_DIM), w_in.dtype)], axis=1).astype(BF16)
    wq = w_uq.reshape(q_rank, heads, HEAD_DIM + ROPE_DIM)
    wq = jnp.pad(wq, ((0, 0), (0, 0), (0, 2 * HEAD_DIM - HEAD_DIM - ROPE_DIM)))
    wq = wq.reshape(q_rank, heads * 2 * HEAD_DIM).astype(BF16)
    wkv = w_ukv.reshape(kv_rank, heads, 2 * HEAD_DIM)
    wk = wkv[:, :, :HEAD_DIM].reshape(kv_rank, heads * HEAD_DIM).astype(BF16)
    wv = wkv[:, :, HEAD_DIM:].reshape(kv_rank, heads * HEAD_DIM).astype(BF16)
    return w_cat, wq, wk, wv


def kernel(x, p, mla_w_in, mla_q_norm, mla_kv_norm, mla_w_uq, mla_w_ukv, mla_w_o, diff_w_in, diff_lambda, diff_sub_norm, diff_w_o, rel_bias, ln_g, ln_b, ffn_w_in, ffn_w_out, ple_w_gate, ple_w_proj):
    batch, seq, d = x.shape
    depth = p.shape[0]
    tokens = batch * seq
    alpha = (2 * depth) ** 0.25
    mla_heads = mla_w_o.shape[1] // HEAD_DIM
    diff_heads = diff_w_o.shape[1] // (2 * HEAD_DIM)
    q_rank, kv_rank = mla_q_norm.shape[1], mla_kv_norm.shape[1]
    mla_scale = (HEAD_DIM + ROPE_DIM) ** -0.5 * LOG2E
    diff_scale = HEAD_DIM ** -0.5 * LOG2E
    diff_q_cols = diff_heads * 2 * HEAD_DIM
    diff_col_scale = jnp.concatenate(
        [jnp.full((1, diff_q_cols), diff_scale, F32),
         jnp.ones((1, diff_w_in.shape[2] - diff_q_cols), F32)], axis=1)

    rope_tabs = _rope_tables(seq)
    rope_tabs_q = tuple(t * mla_scale for t in rope_tabs)
    bias_master = _bias_master(_bias_vector(rel_bias, seq), _tile(seq, 256))

    mla_w_o_b, diff_w_o_b = mla_w_o.astype(BF16), diff_w_o.astype(BF16)
    diff_w_in_b = diff_w_in.astype(BF16)
    ffn_w_in_b, ffn_w_out_b = ffn_w_in.astype(BF16), ffn_w_out.astype(BF16)
    ple_w_gate_b, ple_w_proj_b = ple_w_gate.astype(BF16), ple_w_proj.astype(BF16)
    p_rows = p.reshape(depth, tokens, p.shape[-1])

    xf = x.reshape(tokens, d)
    xb = xf.astype(BF16)
    for i in range(depth):
        j = i // 2
        if i % 2 == 0:
            w_cat, wq, wk, wv = _mla_weights(mla_w_in[j], mla_w_uq[j], mla_w_ukv[j],
                                             mla_heads, q_rank, kv_rank)
            cq, ckv, kr = _mla_in(xb, w_cat, mla_q_norm[j][None], mla_kv_norm[j][None],
                                  rope_tabs, seq)
            q = _mla_q(cq, wq, rope_tabs_q, seq, mla_scale)
            k, v = _mla_kv(ckv, wk, wv, kr)
            o = _mla_attn(q, k, v, batch, seq, mla_heads)
            w_o = mla_w_o_b
        else:
            qkv = _matmul(xb, diff_w_in_b, j, diff_col_scale, BF16)
            o = _diff_attn(qkv, diff_lambda[j], diff_sub_norm[j][None], bias_master,
                           batch, seq, diff_heads, i)
            w_o = diff_w_o_b
        xf, xb = _proj_ln(o, w_o, j, xf, ln_g[i, 0][None], ln_b[i, 0][None], alpha)
        h = _ffn_up(xb, ffn_w_in_b, i)
        xf, xb = _ffn_down_ln(h, ffn_w_out_b, i, xf, ln_g[i, 1][None], ln_b[i, 1][None], alpha)
        xf, xb = _ple(xb, xf, p_rows, ple_w_gate_b, ple_w_proj_b, i)
    return xf.reshape(batch, seq, d)
```

```python
import functools
import math

import jax
import jax.numpy as jnp
from jax import lax
from jax.experimental import pallas as pl
from jax.experimental.pallas import tpu as pltpu

LANES_V7X = 128
VMEM_BYTES_V7X = 64 * 1024 * 1024
VMEM_LIMIT_CAP = VMEM_BYTES_V7X - 8 * 1024 * 1024

HEAD_DIM = 128
ROPE_DIM = 64
ROPE_THETA = 10000.0
REL_BUCKETS = 32
REL_MAX_DIST = 128
LN_EPS = 1e-5
RMS_EPS = 1e-6
LOG2E = math.log2(math.e)

BF16 = jnp.bfloat16
F32 = jnp.float32


def _vmem_limit(block_bytes):
    return int(min(VMEM_LIMIT_CAP, max(32 * 1024 * 1024, 2 * block_bytes + 16 * 1024 * 1024)))


def _nbytes(shape, dtype):
    return math.prod(shape) * jnp.dtype(dtype).itemsize


def _params(semantics, block_bytes):
    return pltpu.CompilerParams(dimension_semantics=semantics,
                                vmem_limit_bytes=_vmem_limit(block_bytes))


def _tile(n, want):
    if n <= want:
        return n
    t = want - want % LANES_V7X
    while t > LANES_V7X and n % t:
        t -= LANES_V7X
    assert n % t == 0, (n, want)
    return t


def _layer_spec(block, index_map, layer, single_buffer=False):
    mode = dict(pipeline_mode=pl.Buffered(1)) if single_buffer else {}
    return pl.BlockSpec((None,) + tuple(block), lambda *g: (layer,) + tuple(index_map(*g)), **mode)


def _dot(a, b):
    return jnp.dot(a, b, preferred_element_type=F32)


def _dot_nt(a, b):
    return lax.dot_general(a, b, (((1,), (1,)), ((), ())), preferred_element_type=F32)


def _layer_norm(y, g, b):
    mu = jnp.mean(y, axis=-1, keepdims=True)
    yc = y - mu
    var = jnp.mean(yc * yc, axis=-1, keepdims=True)
    return yc * lax.rsqrt(var + LN_EPS) * g + b


def _rms_norm(y, g):
    return y * lax.rsqrt(jnp.mean(y * y, axis=-1, keepdims=True) + RMS_EPS) * g


def _rope_half(r, cos_t, sin_a, sin_b):
    up = pltpu.roll(r, LANES_V7X - ROPE_DIM // 2, axis=1)
    down = pltpu.roll(r, ROPE_DIM // 2, axis=1)
    return r * cos_t + up * sin_a + down * sin_b


def _matmul_kernel(x_ref, w_ref, cs_ref, o_ref):
    o_ref[...] = (_dot(x_ref[...], w_ref[...]) * cs_ref[...]).astype(o_ref.dtype)


def _matmul(x, w, layer, col_scale, out_dtype, tm=1024, tn=1024):
    m, k = x.shape
    n = w.shape[2]
    tm, tn = _tile(m, tm), _tile(n, tn)
    blocks = _nbytes((tm, k), x.dtype) + _nbytes((k, tn), w.dtype) + _nbytes((tm, tn), out_dtype)
    return pl.pallas_call(
        _matmul_kernel,
        grid=(m // tm, n // tn),
        in_specs=[pl.BlockSpec((tm, k), lambda i, j: (i, 0)),
                  _layer_spec((k, tn), lambda i, j: (0, j), layer),
                  pl.BlockSpec((1, tn), lambda i, j: (0, j))],
        out_specs=pl.BlockSpec((tm, tn), lambda i, j: (i, j)),
        out_shape=jax.ShapeDtypeStruct((m, n), out_dtype),
        compiler_params=_params(("parallel", "arbitrary"), blocks + _nbytes((tm, tn), F32)),
        name="matmul",
    )(x, w, col_scale)


def _mla_in_kernel(x_ref, w_ref, gq_ref, gkv_ref, cos_ref, sa_ref, sb_ref,
                   cq_ref, ckv_ref, kr_ref, *, q_rank, kv_rank, sub):
    gq, gkv = gq_ref[...], gkv_ref[...]
    for r in range(x_ref.shape[0] // sub):
        rows = slice(r * sub, (r + 1) * sub)
        h = _dot(x_ref[rows, :], w_ref[...])
        cq_ref[rows, :] = _rms_norm(h[:, :q_rank], gq).astype(cq_ref.dtype)
        ckv_ref[rows, :] = _rms_norm(h[:, q_rank:q_rank + kv_rank], gkv).astype(ckv_ref.dtype)
        kr = _rope_half(h[:, q_rank + kv_rank:], cos_ref[rows, :], sa_ref[rows, :], sb_ref[rows, :])
        kr_ref[rows, :] = kr.astype(kr_ref.dtype)


def _mla_in(xb, w_cat, gq, gkv, rope_tabs, seq, tm=1024, sub=256):
    m, k = xb.shape
    q_rank, kv_rank = gq.shape[1], gkv.shape[1]
    n = w_cat.shape[1]
    tm = _tile(seq, tm)
    sub = math.gcd(tm, sub)
    per_seq = seq // tm
    blocks = (_nbytes((tm, k), BF16) + _nbytes((k, n), BF16) + 3 * _nbytes((tm, LANES_V7X), F32)
              + _nbytes((tm, n), BF16))
    tab_spec = pl.BlockSpec((tm, LANES_V7X), lambda i: (i % per_seq, 0))
    return pl.pallas_call(
        functools.partial(_mla_in_kernel, q_rank=q_rank, kv_rank=kv_rank, sub=sub),
        grid=(m // tm,),
        in_specs=[pl.BlockSpec((tm, k), lambda i: (i, 0)),
                  pl.BlockSpec((k, n), lambda i: (0, 0)),
                  pl.BlockSpec((1, q_rank), lambda i: (0, 0)),
                  pl.BlockSpec((1, kv_rank), lambda i: (0, 0)),
                  tab_spec, tab_spec, tab_spec],
        out_specs=[pl.BlockSpec((tm, q_rank), lambda i: (i, 0)),
                   pl.BlockSpec((tm, kv_rank), lambda i: (i, 0)),
                   pl.BlockSpec((tm, LANES_V7X), lambda i: (i, 0))],
        out_shape=[jax.ShapeDtypeStruct((m, q_rank), BF16),
                   jax.ShapeDtypeStruct((m, kv_rank), BF16),
                   jax.ShapeDtypeStruct((m, LANES_V7X), BF16)],
        compiler_params=_params(("parallel",), blocks + _nbytes((tm, n), F32)),
        name="mla_in",
    )(xb, w_cat, gq, gkv, *rope_tabs)


def _mla_q_kernel(c_ref, w_ref, cos_ref, sa_ref, sb_ref, q_ref, *, heads_per_step, scale):
    acc = _dot(c_ref[...], w_ref[...])
    cos_t, sa, sb = cos_ref[...], sa_ref[...], sb_ref[...]
    for h in range(heads_per_step):
        base = 2 * HEAD_DIM * h
        q_ref[:, base:base + HEAD_DIM] = (acc[:, base:base + HEAD_DIM] * scale).astype(q_ref.dtype)
        rope = _rope_half(acc[:, base + HEAD_DIM:base + 2 * HEAD_DIM], cos_t, sa, sb)
        q_ref[:, base + HEAD_DIM:base + 2 * HEAD_DIM] = rope.astype(q_ref.dtype)


def _mla_q(cq, w_uq, rope_tabs, seq, scale, tm=1024, heads_per_step=8):
    m, k = cq.shape
    n = w_uq.shape[1]
    tm = _tile(seq, tm)
    per_seq = seq // tm
    tn = min(n, heads_per_step * 2 * HEAD_DIM)
    heads_per_step = tn // (2 * HEAD_DIM)
    blocks = (_nbytes((tm, k), BF16) + _nbytes((k, tn), BF16) + 3 * _nbytes((tm, LANES_V7X), F32)
              + _nbytes((tm, tn), BF16))
    tab_spec = pl.BlockSpec((tm, LANES_V7X), lambda i, j: (i % per_seq, 0))
    return pl.pallas_call(
        functools.partial(_mla_q_kernel, heads_per_step=heads_per_step, scale=scale),
        grid=(m // tm, n // tn),
        in_specs=[pl.BlockSpec((tm, k), lambda i, j: (i, 0)),
                  pl.BlockSpec((k, tn), lambda i, j: (0, j)),
                  tab_spec, tab_spec, tab_spec],
        out_specs=pl.BlockSpec((tm, tn), lambda i, j: (i, j)),
        out_shape=jax.ShapeDtypeStruct((m, n), BF16),
        compiler_params=_params(("parallel", "arbitrary"), blocks + _nbytes((tm, tn), F32)),
        name="mla_q",
    )(cq, w_uq, *rope_tabs)


def _mla_kv_kernel(c_ref, wk_ref, wv_ref, kr_ref, k_ref, v_ref, *, heads_per_step):
    c = c_ref[...]
    kn = _dot(c, wk_ref[...])
    v_ref[...] = _dot(c, wv_ref[...]).astype(v_ref.dtype)
    kr = kr_ref[...]
    for h in range(heads_per_step):
        base = 2 * HEAD_DIM * h
        k_ref[:, base:base + HEAD_DIM] = kn[:, HEAD_DIM * h:HEAD_DIM * (h + 1)].astype(k_ref.dtype)
        k_ref[:, base + HEAD_DIM:base + 2 * HEAD_DIM] = kr


def _mla_kv(ckv, w_k, w_v, kr, tm=1024, heads_per_step=8):
    m, k = ckv.shape
    n = w_k.shape[1]
    tm = _tile(m, tm)
    tn = min(n, heads_per_step * HEAD_DIM)
    heads_per_step = tn // HEAD_DIM
    blocks = (_nbytes((tm, k), BF16) + 2 * _nbytes((k, tn), BF16) + _nbytes((tm, LANES_V7X), BF16)
              + 3 * _nbytes((tm, tn), BF16))
    return pl.pallas_call(
        functools.partial(_mla_kv_kernel, heads_per_step=heads_per_step),
        grid=(m // tm, n // tn),
        in_specs=[pl.BlockSpec((tm, k), lambda i, j: (i, 0)),
                  pl.BlockSpec((k, tn), lambda i, j: (0, j)),
                  pl.BlockSpec((k, tn), lambda i, j: (0, j)),
                  pl.BlockSpec((tm, LANES_V7X), lambda i, j: (i, 0))],
        out_specs=[pl.BlockSpec((tm, 2 * tn), lambda i, j: (i, j)),
                   pl.BlockSpec((tm, tn), lambda i, j: (i, j))],
        out_shape=[jax.ShapeDtypeStruct((m, 2 * n), BF16),
                   jax.ShapeDtypeStruct((m, n), BF16)],
        compiler_params=_params(("parallel", "arbitrary"), blocks + 2 * _nbytes((tm, tn), F32)),
        name="mla_kv",
    )(ckv, w_k, w_v, kr)


def _online_softmax_pv(q, k_ref, v_ref, k_cols, tk, bias_fn=None):
    tq = q.shape[0]
    seq = k_ref.shape[0]
    groups = tk // LANES_V7X
    m = jnp.full((tq, 1), -jnp.inf, F32)
    l_acc = jnp.zeros((tq, LANES_V7X), F32)
    acc = jnp.zeros((tq, v_ref.shape[1]), F32)
    for c in range(seq // tk):
        s = _dot_nt(q, k_ref[c * tk:(c + 1) * tk, k_cols])
        if bias_fn is not None:
            s = s + bias_fn(c)
        m_lane = s[:, :LANES_V7X]
        for g in range(1, groups):
            m_lane = jnp.maximum(m_lane, s[:, g * LANES_V7X:(g + 1) * LANES_V7X])
        m_new = jnp.maximum(m, jnp.max(m_lane, axis=-1, keepdims=True))
        alpha = jnp.exp2(m - m_new)
        p = jnp.exp2(s - m_new)
        p_sum = p[:, :LANES_V7X]
        for g in range(1, groups):
            p_sum = p_sum + p[:, g * LANES_V7X:(g + 1) * LANES_V7X]
        l_acc = alpha * l_acc + p_sum
        acc = alpha * acc + _dot(p.astype(BF16), v_ref[c * tk:(c + 1) * tk, :])
        m = m_new
    return acc, jnp.sum(l_acc, axis=-1, keepdims=True)


def _mla_attn_kernel(q_ref, k_ref, v_ref, o_ref, *, tk):
    acc, l = _online_softmax_pv(q_ref[...], k_ref, v_ref, slice(None), tk)
    o_ref[...] = (acc / l).astype(o_ref.dtype)


def _mla_attn(q, k, v, batch, seq, heads, tq=2048, tk=256):
    tq, tk = _tile(seq, tq), _tile(seq, tk)
    nq = seq // tq
    dk, dv = 2 * HEAD_DIM, HEAD_DIM
    blocks = (_nbytes((tq, dk), BF16) + _nbytes((seq, dk), BF16) + _nbytes((seq, dv), BF16)
              + _nbytes((tq, dv), BF16))
    return pl.pallas_call(
        functools.partial(_mla_attn_kernel, tk=tk),
        grid=(batch, heads, nq),
        in_specs=[pl.BlockSpec((tq, dk), lambda b, h, i: (b * nq + i, h)),
                  pl.BlockSpec((seq, dk), lambda b, h, i: (b, h)),
                  pl.BlockSpec((seq, dv), lambda b, h, i: (b, h))],
        out_specs=pl.BlockSpec((tq, dv), lambda b, h, i: (b * nq + i, h)),
        out_shape=jax.ShapeDtypeStruct((batch * seq, heads * dv), BF16),
        compiler_params=_params(("parallel", "parallel", "arbitrary"),
                                blocks + 4 * _nbytes((tq, tk), F32)),
        name="mla_attn",
    )(q, k, v)


def _bias_master_kernel(vec_ref, o_ref, *, tq):
    width = vec_ref.shape[-1]
    m = jnp.broadcast_to(vec_ref[0], (tq, width))
    left = (tq - 1) - lax.broadcasted_iota(jnp.int32, (tq, 1), 0)
    step = 1
    while step < tq:
        shifted = pltpu.roll(m, width - step, axis=1)
        m = jnp.where((left & step) != 0, shifted, m)
        step *= 2
    o_ref[0] = m


def _bias_master(vec, tq):
    heads, _, width = vec.shape
    blocks = _nbytes((1, width), F32) + _nbytes((tq, width), F32)
    return pl.pallas_call(
        functools.partial(_bias_master_kernel, tq=tq),
        grid=(heads,),
        in_specs=[pl.BlockSpec((1, 1, width), lambda h: (h, 0, 0))],
        out_specs=pl.BlockSpec((1, tq, width), lambda h: (h, 0, 0)),
        out_shape=jax.ShapeDtypeStruct((heads, tq, width), F32),
        compiler_params=_params(("parallel",), blocks + 2 * _nbytes((tq, width), F32)),
        name="bias_master",
    )(vec)


def _diff_attn_kernel(lam_ref, g_ref, q_ref, k_ref, v_ref, bias_ref, o_ref, z_ref, m_ref,
                      *, tk, nq, n_tiles, lambda_init):
    t = pl.program_id(0)
    tq = q_ref.shape[0]
    seq = k_ref.shape[0]
    sub = bias_ref.shape[1]
    groups = tk // LANES_V7X
    lam = lam_ref[...]
    lam_full = (jnp.exp(jnp.sum(lam[0:1] * lam[1:2], axis=-1, keepdims=True))
                - jnp.exp(jnp.sum(lam[2:3] * lam[3:4], axis=-1, keepdims=True)) + lambda_init)
    row0 = (jnp.minimum(t, n_tiles - 1) % nq) * tq

    @pl.when(t == 0)
    def _():
        z_ref[...] = jnp.zeros(z_ref.shape, F32)
        m_ref[...] = jnp.zeros(m_ref.shape, F32)

    def bias_fn(c):
        parts = []
        for j in range(tq // sub):
            off = pl.multiple_of(seq - sub - row0 - j * sub + c * tk, LANES_V7X)
            parts.append(bias_ref[0, :, pl.ds(off, tk)])
        return parts[0] if len(parts) == 1 else jnp.concatenate(parts, axis=0)

    qs = [q_ref[:, g * HEAD_DIM:(g + 1) * HEAD_DIM] for g in range(2)]
    ms = [m_ref[g] for g in range(2)]
    m_lane = [jnp.full((tq, LANES_V7X), -jnp.inf, F32) for _ in range(2)]
    l_acc = [jnp.zeros((tq, LANES_V7X), F32) for _ in range(2)]
    acc = [jnp.zeros((tq, v_ref.shape[1]), F32) for _ in range(2)]
    for c in range(seq // tk):
        bias = bias_fn(c)
        for g in range(2):
            ps = []
            for j in range(groups):
                lo = c * tk + j * LANES_V7X
                pg = jnp.exp2(z_ref[g, :, lo:lo + LANES_V7X] - ms[g])
                l_acc[g] = l_acc[g] + pg
                ps.append(pg.astype(BF16))
            acc[g] = acc[g] + _dot(jnp.concatenate(ps, axis=1), v_ref[c * tk:(c + 1) * tk, :])
            z = _dot_nt(qs[g], k_ref[c * tk:(c + 1) * tk, g * HEAD_DIM:(g + 1) * HEAD_DIM]) + bias
            z_ref[g, :, c * tk:(c + 1) * tk] = z
            for j in range(groups):
                m_lane[g] = jnp.maximum(m_lane[g], z[:, j * LANES_V7X:(j + 1) * LANES_V7X])
    outs = []
    for g in range(2):
        m_ref[g] = jnp.broadcast_to(jnp.max(m_lane[g], axis=-1, keepdims=True), (tq, LANES_V7X))
        outs.append(acc[g] / jnp.sum(l_acc[g], axis=-1, keepdims=True))
    o = outs[0] - lam_full * outs[1]
    o = _rms_norm(o, g_ref[...]) * (1.0 - lambda_init)
    o_ref[...] = o.astype(o_ref.dtype)


def _diff_attn(qkv, lam, sub_norm, bias_master, batch, seq, heads, layer_idx, tq=512, tk=256):
    sub = bias_master.shape[1]
    tq, tk = _tile(seq, tq), _tile(seq, tk)
    assert tq % sub == 0 and sub % LANES_V7X == 0, (tq, sub)
    nq = seq // tq
    d2 = 2 * HEAD_DIM
    n_tiles = heads * batch * nq
    lambda_init = 0.8 - 0.6 * math.exp(-0.3 * layer_idx)

    def tile(u):
        return u // (batch * nq), (u // nq) % batch, u % nq

    def prepared(t):
        return tile(jnp.minimum(t, n_tiles - 1))

    def finished(t):
        return tile(jnp.maximum(t - 1, 0))

    def q_map(t):
        h, b, i = prepared(t)
        return (b * nq + i, h)

    def k_map(t):
        h, b, i = prepared(t)
        return (b, heads + h)

    def bias_map(t):
        h, b, i = prepared(t)
        return (h, 0, 0)

    def v_map(t):
        h, b, i = finished(t)
        return (b, 2 * heads + h)

    def o_map(t):
        h, b, i = finished(t)
        return (b * nq + i, h)

    blocks = (2 * _nbytes((tq, d2), BF16) + 2 * _nbytes((seq, d2), BF16)
              + _nbytes((sub, 2 * seq), F32))
    scratch = _nbytes((2, tq, seq), F32) + _nbytes((2, tq, LANES_V7X), F32)
    return pl.pallas_call(
        functools.partial(_diff_attn_kernel, tk=tk, nq=nq, n_tiles=n_tiles, lambda_init=lambda_init),
        grid=(n_tiles + 1,),
        in_specs=[pl.BlockSpec(lam.shape, lambda t: (0, 0)),
                  pl.BlockSpec((1, d2), lambda t: (0, 0)),
                  pl.BlockSpec((tq, d2), q_map),
                  pl.BlockSpec((seq, d2), k_map),
                  pl.BlockSpec((seq, d2), v_map),
                  pl.BlockSpec((1, sub, 2 * seq), bias_map)],
        out_specs=pl.BlockSpec((tq, d2), o_map),
        out_shape=jax.ShapeDtypeStruct((batch * seq, heads * d2), BF16),
        scratch_shapes=[pltpu.VMEM((2, tq, seq), F32), pltpu.VMEM((2, tq, LANES_V7X), F32)],
        compiler_params=_params(("arbitrary",), blocks + scratch // 2 + 2 * _nbytes((tq, tk), F32)),
        name="diff_attn",
    )(lam, sub_norm, qkv, qkv, qkv, bias_master)


def _proj_ln_kernel(a_ref, w_ref, x_ref, g_ref, b_ref, y_ref, yb_ref, *, alpha, sub):
    g, b = g_ref[...], b_ref[...]
    for r in range(a_ref.shape[0] // sub):
        rows = slice(r * sub, (r + 1) * sub)
        y = alpha * x_ref[rows, :] + _dot(a_ref[rows, :], w_ref[...])
        y = _layer_norm(y, g, b)
        y_ref[rows, :] = y
        yb_ref[rows, :] = y.astype(yb_ref.dtype)


def _proj_ln(a, w, layer, x, g, b, alpha, tm=512, sub=256, name="proj_ln"):
    m, k = a.shape
    d = w.shape[2]
    tm = _tile(m, tm)
    sub = math.gcd(tm, sub)
    streamed = (_nbytes((tm, k), BF16) + 2 * _nbytes((tm, d), F32) + _nbytes((tm, d), BF16))
    resident = _nbytes((k, d), BF16)
    row = lambda i: (i, 0)
    fixed = lambda i: (0, 0)
    return pl.pallas_call(
        functools.partial(_proj_ln_kernel, alpha=alpha, sub=sub),
        grid=(m // tm,),
        in_specs=[pl.BlockSpec((tm, k), row), _layer_spec((k, d), fixed, layer, single_buffer=True),
                  pl.BlockSpec((tm, d), row), pl.BlockSpec((1, d), fixed),
                  pl.BlockSpec((1, d), fixed)],
        out_specs=[pl.BlockSpec((tm, d), row), pl.BlockSpec((tm, d), row)],
        out_shape=[jax.ShapeDtypeStruct((m, d), F32), jax.ShapeDtypeStruct((m, d), BF16)],
        compiler_params=_params(("parallel",), streamed + resident // 2 + 2 * _nbytes((sub, d), F32)),
        name=name,
    )(a, w, x, g, b)


def _ffn_up_kernel(x_ref, wg_ref, wu_ref, h_ref):
    x = x_ref[...]
    g = _dot(x, wg_ref[...])
    u = _dot(x, wu_ref[...])
    h_ref[...] = (g * jax.nn.sigmoid(g) * u).astype(h_ref.dtype)


def _ffn_up(xb, w_in, layer, tm=1024, tn=512):
    m, k = xb.shape
    ff = w_in.shape[2] // 2
    tm, tn = _tile(m, tm), _tile(ff, tn)
    n_tiles = ff // tn
    blocks = _nbytes((tm, k), BF16) + 2 * _nbytes((k, tn), BF16) + _nbytes((tm, tn), BF16)
    return pl.pallas_call(
        _ffn_up_kernel,
        grid=(m // tm, n_tiles),
        in_specs=[pl.BlockSpec((tm, k), lambda i, j: (i, 0)),
                  _layer_spec((k, tn), lambda i, j: (0, j), layer),
                  _layer_spec((k, tn), lambda i, j: (0, j + n_tiles), layer)],
        out_specs=pl.BlockSpec((tm, tn), lambda i, j: (i, j)),
        out_shape=jax.ShapeDtypeStruct((m, ff), BF16),
        compiler_params=_params(("parallel", "arbitrary"), blocks + 3 * _nbytes((tm, tn), F32)),
        name="ffn_up",
    )(xb, w_in, w_in)


def _ple_kernel(xb_ref, wg_ref, p_ref, wp_ref, x_ref, y_ref, yb_ref):
    gate = jax.nn.sigmoid(_dot(xb_ref[...], wg_ref[...]))
    proj = _dot(p_ref[...].astype(BF16), wp_ref[...])
    y = x_ref[...] + gate * proj
    y_ref[...] = y
    yb_ref[...] = y.astype(yb_ref.dtype)


def _ple(xb, x, p, w_gate, w_proj, layer, tm=512):
    m, d = x.shape
    pd = p.shape[2]
    tm = _tile(m, tm)
    blocks = (_nbytes((tm, d), BF16) + _nbytes((d, d), BF16) + _nbytes((tm, pd), F32)
              + _nbytes((pd, d), BF16) + 2 * _nbytes((tm, d), F32) + _nbytes((tm, d), BF16))
    row = lambda i: (i, 0)
    fixed = lambda i: (0, 0)
    return pl.pallas_call(
        _ple_kernel,
        grid=(m // tm,),
        in_specs=[pl.BlockSpec((tm, d), row),
                  _layer_spec((d, d), fixed, layer),
                  _layer_spec((tm, pd), row, layer),
                  _layer_spec((pd, d), fixed, layer),
                  pl.BlockSpec((tm, d), row)],
        out_specs=[pl.BlockSpec((tm, d), row), pl.BlockSpec((tm, d), row)],
        out_shape=[jax.ShapeDtypeStruct((m, d), F32), jax.ShapeDtypeStruct((m, d), BF16)],
        compiler_params=_params(("parallel",), blocks + 2 * _nbytes((tm, d), F32)),
        name="ple",
    )(xb, w_gate, p, w_proj, x)


def _rope_tables(seq):
    half = ROPE_DIM // 2
    pos = jnp.arange(seq, dtype=F32)
    inv = 1.0 / (ROPE_THETA ** (jnp.arange(0, ROPE_DIM, 2, dtype=F32) / ROPE_DIM))
    ang = pos[:, None] * inv[None, :]
    cos, sin = jnp.cos(ang), jnp.sin(ang)
    zeros = jnp.zeros((seq, half), F32)
    pad = jnp.zeros((seq, LANES_V7X - ROPE_DIM), F32)
    cos_t = jnp.concatenate([cos, cos, pad], axis=1)
    sin_a = jnp.concatenate([-sin, zeros, pad], axis=1)
    sin_b = jnp.concatenate([zeros, sin, pad], axis=1)
    return cos_t, sin_a, sin_b


def _t5_bucket(rel):
    nb = REL_BUCKETS // 2
    max_exact = nb // 2
    ret = (rel > 0).astype(jnp.int32) * nb
    n = jnp.abs(rel)
    nf = jnp.maximum(n, 1).astype(F32)
    large = max_exact + (jnp.log(nf / max_exact) / math.log(REL_MAX_DIST / max_exact)
                         * (nb - max_exact)).astype(jnp.int32)
    large = jnp.minimum(large, nb - 1)
    return ret + jnp.where(n < max_exact, n, large)


def _bias_vector(rel_bias, seq):
    rel = jnp.arange(-(seq - 1), seq + 1, dtype=jnp.int32)
    vec = rel_bias.astype(F32)[_t5_bucket(rel)] * LOG2E
    return jnp.transpose(vec)[:, None, :]


def _mla_weights(w_in, w_uq, w_ukv, heads, q_rank, kv_rank):
    d = w_in.shape[0]
    w_cat = jnp.concatenate(
        [w_in, jnp.zeros((d, LANES_V7X - ROPE_DIM), w_in.dtype)], axis=1).astype(BF16)
    wq = w_uq.reshape(q_rank, heads, HEAD_DIM + ROPE_DIM)
    wq = jnp.pad(wq, ((0, 0), (0, 0), (0, 2 * HEAD_DIM - HEAD_DIM - ROPE_DIM)))
    wq = wq.reshape(q_rank, heads * 2 * HEAD_DIM).astype(BF16)
    wkv = w_ukv.reshape(kv_rank, heads, 2 * HEAD_DIM)
    wk = wkv[:, :, :HEAD_DIM].reshape(kv_rank, heads * HEAD_DIM).astype(BF16)
    wv = wkv[:, :, HEAD_DIM:].reshape(kv_rank, heads * HEAD_DIM).astype(BF16)
    return w_cat, wq, wk, wv


def kernel(x, p, mla_w_in, mla_q_norm, mla_kv_norm, mla_w_uq, mla_w_ukv, mla_w_o, diff_w_in, diff_lambda, diff_sub_norm, diff_w_o, rel_bias, ln_g, ln_b, ffn_w_in, ffn_w_out, ple_w_gate, ple_w_proj):
    batch, seq, d = x.shape
    depth = p.shape[0]
    tokens = batch * seq
    alpha = (2 * depth) ** 0.25
    mla_heads = mla_w_o.shape[1] // HEAD_DIM
    diff_heads = diff_w_o.shape[1] // (2 * HEAD_DIM)
    q_rank, kv_rank = mla_q_norm.shape[1], mla_kv_norm.shape[1]
    mla_scale = (HEAD_DIM + ROPE_DIM) ** -0.5 * LOG2E
    diff_scale = HEAD_DIM ** -0.5 * LOG2E
    diff_q_cols = diff_heads * 2 * HEAD_DIM
    diff_col_scale = jnp.concatenate(
        [jnp.full((1, diff_q_cols), diff_scale, F32),
         jnp.ones((1, diff_w_in.shape[2] - diff_q_cols), F32)], axis=1)

    rope_tabs = _rope_tables(seq)
    rope_tabs_q = tuple(t * mla_scale for t in rope_tabs)
    bias_master = _bias_master(_bias_vector(rel_bias, seq), _tile(seq, 256))

    mla_w_o_b, diff_w_o_b = mla_w_o.astype(BF16), diff_w_o.astype(BF16)
    diff_w_in_b = diff_w_in.astype(BF16)
    ffn_w_in_b, ffn_w_out_b = ffn_w_in.astype(BF16), ffn_w_out.astype(BF16)
    ple_w_gate_b, ple_w_proj_b = ple_w_gate.astype(BF16), ple_w_proj.astype(BF16)
    p_rows = p.reshape(depth, tokens, p.shape[-1])

    xf = x.reshape(tokens, d)
    xb = xf.astype(BF16)
    for i in range(depth):
        j = i // 2
        if i % 2 == 0:
            w_cat, wq, wk, wv = _mla_weights(mla_w_in[j], mla_w_uq[j], mla_w_ukv[j],
                                             mla_heads, q_rank, kv_rank)
            cq, ckv, kr = _mla_in(xb, w_cat, mla_q_norm[j][None], mla_kv_norm[j][None],
                                  rope_tabs, seq)
            q = _mla_q(cq, wq, rope_tabs_q, seq, mla_scale)
            k, v = _mla_kv(ckv, wk, wv, kr)
            o = _mla_attn(q, k, v, batch, seq, mla_heads)
            w_o = mla_w_o_b
        else:
            qkv = _matmul(xb, diff_w_in_b, j, diff_col_scale, BF16)
            o = _diff_attn(qkv, diff_lambda[j], diff_sub_norm[j][None], bias_master,
                           batch, seq, diff_heads, i)
            w_o = diff_w_o_b
        xf, xb = _proj_ln(o, w_o, j, xf, ln_g[i, 0][None], ln_b[i, 0][None], alpha)
        h = _ffn_up(xb, ffn_w_in_b, i)
        xf, xb = _proj_ln(h, ffn_w_out_b, i, xf, ln_g[i, 1][None], ln_b[i, 1][None], alpha,
                          tm=256, sub=128, name="ffn_down_ln")
        xf, xb = _ple(xb, xf, p_rows, ple_w_gate_b, ple_w_proj_b, i)
    return xf.reshape(batch, seq, d)
```
